```python
import math
import jax, jax.numpy as jnp
from jax import lax
import numpy as np

D_MODEL = 1024
BATCH = 1
SEQ = 16384
DEPTH = 1

CHUNK = 64
HEAD_DIM = 64
ROPE_THETA = 10000.0
SWA_Q_HEADS = (D_MODEL // 2) // HEAD_DIM
SWA_KV_HEADS = max(1, SWA_Q_HEADS // 4)
SWA_WINDOW = 128
SWA_WIN_CHUNKS = -(-SWA_WINDOW // CHUNK)
DIFF_V_DIM = 2 * HEAD_DIM
DIFF_HEADS = (D_MODEL // 2) // DIFF_V_DIM
DIFF_Q_BLOCK = 128
IN_SPLITS = (SWA_Q_HEADS * HEAD_DIM, SWA_KV_HEADS * HEAD_DIM, SWA_KV_HEADS * HEAD_DIM,
             DIFF_HEADS * 2 * HEAD_DIM, DIFF_HEADS * 2 * HEAD_DIM, DIFF_HEADS * DIFF_V_DIM)
IN_WIDTH = sum(IN_SPLITS)
MIX_WIDTH = SWA_Q_HEADS * HEAD_DIM + DIFF_HEADS * DIFF_V_DIM
N_EXPERTS = 256
TOP_K = 8
N_GROUPS = 8
TOPK_GROUPS = 4
EXPERT_FF = 256
SHARED_FF = 256
ROUTED_SCALE = 2.5
MOE_BLOCK = 128
DEEPNORM_ALPHA = (2.0 * DEPTH) ** 0.25
DEEPNORM_BETA = (8.0 * DEPTH) ** -0.25
NORM_EPS = 1e-5

kernel_name = "hybrid_swa_sink_diffattn_moe_deepnorm"


def layer_norm(x, g, b):
    xf = x.astype(jnp.float32)
    mu = xf.mean(-1, keepdims=True)
    var = jnp.square(xf - mu).mean(-1, keepdims=True)
    y = (xf - mu) * lax.rsqrt(var + NORM_EPS) * g.astype(jnp.float32) + b.astype(jnp.float32)
    return y.astype(x.dtype)


def rms_norm(x, g):
    xf = x.astype(jnp.float32)
    y = xf * lax.rsqrt(jnp.square(xf).mean(-1, keepdims=True) + NORM_EPS) * g.astype(jnp.float32)
    return y.astype(x.dtype)


def rope_tables(seq):
    inv = 1.0 / (ROPE_THETA ** (jnp.arange(0, HEAD_DIM, 2, dtype=jnp.float32) / HEAD_DIM))
    ang = jnp.arange(seq, dtype=jnp.float32)[:, None] * inv[None, :]
    return jnp.cos(ang), jnp.sin(ang)


def apply_rope(t, cos, sin):
    half = t.shape[-1] // 2
    shp = (1, t.shape[1]) + (1,) * (t.ndim - 3) + (half,)
    c, s = cos.reshape(shp), sin.reshape(shp)
    tf = t.astype(jnp.float32)
    t1, t2 = tf[..., :half], tf[..., half:]
    return jnp.concatenate([t1 * c - t2 * s, t2 * c + t1 * s], axis=-1).astype(t.dtype)


def swa_sink_attention(q, k, v, sinks):
    B, S, Hq, d = q.shape
    Hkv = k.shape[2]
    G = Hq // Hkv
    NC = S // CHUNK
    W = SWA_WIN_CHUNKS
    qc = q.reshape(B, NC, CHUNK, Hkv, G, d)

    def windows(t):
        tc = t.reshape(B, NC, CHUNK, Hkv, d)
        tp = jnp.pad(tc, ((0, 0), (W, 0), (0, 0), (0, 0), (0, 0)))
        return jnp.concatenate([tp[:, i:i + NC] for i in range(W + 1)], axis=2)

    kw, vw = windows(k), windows(v)
    key_chunk = jnp.arange(NC)[:, None] - W + jnp.arange((W + 1) * CHUNK)[None, :] // CHUNK
    valid = key_chunk >= 0
    s = jnp.einsum('bnqkgd,bnskd->bnkgqs', qc, kw).astype(jnp.float32) * (d ** -0.5)
    s = jnp.where(valid[None, :, None, None, None, :], s, -jnp.inf)
    sink = sinks.astype(jnp.float32).reshape(1, 1, Hkv, G, 1, 1)
    m = jnp.maximum(s.max(-1, keepdims=True), sink)
    p = jnp.exp(s - m)
    p = (p / (p.sum(-1, keepdims=True) + jnp.exp(sink - m))).astype(v.dtype)
    o = jnp.einsum('bnkgqs,bnskd->bnqkgd', p, vw)
    return o.reshape(B, S, Hq * d)


def diff_attention(q, k, v, lam, lam_init, sub_g):
    B, S, H, _, d = q.shape
    NQ = S // DIFF_Q_BLOCK
    qb = q.reshape(B, NQ, DIFF_Q_BLOCK, H, 2, d).transpose(1, 0, 2, 3, 4, 5)
    key_chunk = jnp.arange(S) // CHUNK
    scale = d ** -0.5

    def block(args):
        qi, i = args
        q_chunk = (i * DIFF_Q_BLOCK + jnp.arange(DIFF_Q_BLOCK)) // CHUNK
        mask = key_chunk[None, :] <= q_chunk[:, None]
        s = jnp.einsum('bqhmd,bshmd->bhmqs', qi, k).astype(jnp.float32) * scale
        a = jax.nn.softmax(jnp.where(mask, s, -jnp.inf), axis=-1)
        w = (a[:, :, 0] - lam * a[:, :, 1]).astype(v.dtype)
        return jnp.einsum('bhqs,bshe->bqhe', w, v)

    o = lax.map(block, (qb, jnp.arange(NQ)))
    o = o.transpose(1, 0, 2, 3, 4).reshape(B, S, H, 2 * d)
    o = rms_norm(o, sub_g) * (1.0 - lam_init)
    return o.reshape(B, S, H * 2 * d)


def hybrid_mixer(x, w_in, b_in, sinks, lam_q1, lam_k1, lam_q2, lam_k2, sub_g, w_out, lam_init, cos, sin):
    B, S, _ = x.shape
    proj = x @ w_in + b_in
    offs = np.cumsum(IN_SPLITS)[:-1].tolist()
    qa, ka, va, qd, kd, vd = jnp.split(proj, offs, axis=-1)
    qa = apply_rope(qa.reshape(B, S, SWA_Q_HEADS, HEAD_DIM), cos, sin)
    ka = apply_rope(ka.reshape(B, S, SWA_KV_HEADS, HEAD_DIM), cos, sin)
    va = va.reshape(B, S, SWA_KV_HEADS, HEAD_DIM)
    out_a = swa_sink_attention(qa, ka, va, sinks)

    qd = apply_rope(qd.reshape(B, S, DIFF_HEADS, 2, HEAD_DIM), cos, sin)
    kd = apply_rope(kd.reshape(B, S, DIFF_HEADS, 2, HEAD_DIM), cos, sin)
    vd = vd.reshape(B, S, DIFF_HEADS, DIFF_V_DIM)
    lam = (jnp.exp(jnp.sum(lam_q1.astype(jnp.float32) * lam_k1.astype(jnp.float32)))
           - jnp.exp(jnp.sum(lam_q2.astype(jnp.float32) * lam_k2.astype(jnp.float32))) + lam_init)
    out_b = diff_attention(qd, kd, vd, lam, lam_init, sub_g)
    return jnp.concatenate([out_a, out_b], axis=-1) @ w_out


def swiglu(h, wg, wu, wd):
    return (jax.nn.silu(h @ wg) * (h @ wu)) @ wd


def moe(h, w_router, r_bias, w_g, w_u, w_d, ws_g, ws_u, ws_d):
    B, S, D = h.shape
    T = B * S
    hf = h.reshape(T, D)
    scores = jax.nn.sigmoid((hf @ w_router).astype(jnp.float32))
    sel = scores + r_bias.astype(jnp.float32)
    grp = sel.reshape(T, N_GROUPS, N_EXPERTS // N_GROUPS)
    grp_score = lax.top_k(grp, 2)[0].sum(-1)
    top_g = lax.top_k(grp_score, TOPK_GROUPS)[1]
    gmask = jnp.any(top_g[:, :, None] == jnp.arange(N_GROUPS)[None, None, :], axis=1)
    sel = jnp.where(jnp.repeat(gmask, N_EXPERTS // N_GROUPS, axis=1), sel, -jnp.inf)
    top_e = lax.top_k(sel, TOP_K)[1]
    gate = jnp.take_along_axis(scores, top_e, axis=1)
    gate = gate / gate.sum(-1, keepdims=True) * ROUTED_SCALE

    A = T * TOP_K
    NB = (A + N_EXPERTS * (MOE_BLOCK - 1) + MOE_BLOCK - 1) // MOE_BLOCK
    P = NB * MOE_BLOCK
    e_flat = top_e.reshape(A)
    tok_flat = jnp.repeat(jnp.arange(T, dtype=jnp.int32), TOP_K)
    g_flat = gate.reshape(A)
    order = jnp.argsort(e_flat)
    e_s, tok_s, g_s = e_flat[order], tok_flat[order], g_flat[order]
    counts = jnp.bincount(e_flat, length=N_EXPERTS)
    padded = (counts + MOE_BLOCK - 1) // MOE_BLOCK * MOE_BLOCK
    starts = jnp.cumsum(counts) - counts
    pends = jnp.cumsum(padded)
    pstarts = pends - padded
    dest = pstarts[e_s] + (jnp.arange(A) - starts[e_s])
    tok_buf = jnp.zeros((P,), jnp.int32).at[dest].set(tok_s)
    g_buf = jnp.zeros((P,), hf.dtype).at[dest].set(g_s.astype(hf.dtype))
    block_e = jnp.minimum(jnp.searchsorted(pends, jnp.arange(NB) * MOE_BLOCK, side='right'), N_EXPERTS - 1)

    def expert_block(args):
        tok, g, e = args
        xb = hf[tok]
        return swiglu(xb, w_g[e], w_u[e], w_d[e]) * g[:, None]

    y = lax.map(expert_block, (tok_buf.reshape(NB, MOE_BLOCK), g_buf.reshape(NB, MOE_BLOCK), block_e))
    routed = jnp.zeros((T, D), hf.dtype).at[tok_buf].add(y.reshape(P, D))
    return (routed + swiglu(hf, ws_g, ws_u, ws_d)).reshape(B, S, D)


def setup_inputs(seed: int = 0) -> dict:
    key = jax.random.key(seed)
    ks = jax.random.split(key, 24)
    L, D, E = DEPTH, D_MODEL, N_EXPERTS

    def nrm(k, shape, scale):
        return jax.random.normal(k, shape, jnp.float32) * scale

    col_scale = jnp.asarray(np.concatenate([
        np.full((w,), DEEPNORM_BETA if i in (2, 5) else 1.0, np.float32) for i, w in enumerate(IN_SPLITS)]))
    return {
        "x": nrm(ks[0], (BATCH, SEQ, D), 1.0),
        "w_in": nrm(ks[1], (L, D, IN_WIDTH), D ** -0.5) * col_scale,
        "b_in": nrm(ks[2], (L, IN_WIDTH), 0.02),
        "sinks": nrm(ks[3], (L, SWA_Q_HEADS), 1.0),
        "lam_q1": nrm(ks[4], (L, HEAD_DIM), 0.1),
        "lam_k1": nrm(ks[5], (L, HEAD_DIM), 0.1),
        "lam_q2": nrm(ks[6], (L, HEAD_DIM), 0.1),
        "lam_k2": nrm(ks[7], (L, HEAD_DIM), 0.1),
        "diff_norm_g": 1.0 + nrm(ks[8], (L, DIFF_V_DIM), 0.02),
        "w_out": nrm(ks[9], (L, MIX_WIDTH, D), MIX_WIDTH ** -0.5 * DEEPNORM_BETA),
        "ln1_g": 1.0 + nrm(ks[10], (L, D), 0.02),
        "ln1_b": nrm(ks[11], (L, D), 0.02),
        "w_router": nrm(ks[12], (L, D, E), D ** -0.5),
        "router_bias": nrm(ks[13], (L, E), 0.01),
        "w_gate": nrm(ks[14], (L, E, D, EXPERT_FF), D ** -0.5),
        "w_up": nrm(ks[15], (L, E, D, EXPERT_FF), D ** -0.5),
        "w_down": nrm(ks[16], (L, E, EXPERT_FF, D), EXPERT_FF ** -0.5 * DEEPNORM_BETA),
        "ws_gate": nrm(ks[17], (L, D, SHARED_FF), D ** -0.5),
        "ws_up": nrm(ks[18], (L, D, SHARED_FF), D ** -0.5),
        "ws_down": nrm(ks[19], (L, SHARED_FF, D), SHARED_FF ** -0.5 * DEEPNORM_BETA),
        "ln2_g": 1.0 + nrm(ks[20], (L, D), 0.02),
        "ln2_b": nrm(ks[21], (L, D), 0.02),
    }


def reference(x, w_in, b_in, sinks, lam_q1, lam_k1, lam_q2, lam_k2, diff_norm_g, w_out, ln1_g, ln1_b,
              w_router, router_bias, w_gate, w_up, w_down, ws_gate, ws_up, ws_down, ln2_g, ln2_b):
    cos, sin = rope_tables(x.shape[1])
    for l in range(DEPTH):
        lam_init = 0.8 - 0.6 * math.exp(-0.3 * l)
        mix = hybrid_mixer(x, w_in[l], b_in[l], sinks[l], lam_q1[l], lam_k1[l], lam_q2[l], lam_k2[l],
                           diff_norm_g[l], w_out[l], lam_init, cos, sin)
        x = layer_norm(DEEPNORM_ALPHA * x + mix, ln1_g[l], ln1_b[l])
        ffn = moe(x, w_router[l], router_bias[l], w_gate[l], w_up[l], w_down[l], ws_gate[l], ws_up[l], ws_down[l])
        x = layer_norm(DEEPNORM_ALPHA * x + ffn, ln2_g[l], ln2_b[l])
    return x
```

```python
import functools
import math

import numpy as np
import jax
import jax.numpy as jnp
from jax import lax
from jax.experimental import pallas as pl
from jax.experimental.pallas import tpu as pltpu

D_MODEL = 1024
CHUNK = 64
HEAD_DIM = 64
ROPE_THETA = 10000.0
SWA_Q_HEADS = 8
SWA_KV_HEADS = 2
SWA_WIN_CHUNKS = 2
DIFF_HEADS = 4
DIFF_V_DIM = 2 * HEAD_DIM
N_EXPERTS = 256
TOP_K = 8
N_GROUPS = 8
TOPK_GROUPS = 4
GROUP_SIZE = N_EXPERTS // N_GROUPS
EXPERT_FF = 256
SHARED_FF = 256
ROUTED_SCALE = 2.5
MOE_BLOCK = 128
DEPTH = 1
DEEPNORM_ALPHA = (2.0 * DEPTH) ** 0.25
NORM_EPS = 1e-5
LOG2E = math.log2(math.e)
NEG_BIG = -1e30
LANES = 128

_AQ, _AK, _AV, _BQ, _BK, _BV, _END = 0, 512, 640, 768, 1280, 1792, 2304
IN_WIDTH = _END

_NT = (((1,), (1,)), ((), ()))

F32 = jnp.float32
BF16 = jnp.bfloat16


def _cparams(vmem_mb, n_axes=1):
    return pltpu.CompilerParams(dimension_semantics=("arbitrary",) * n_axes,
                                vmem_limit_bytes=vmem_mb * 1024 * 1024)


def _in_proj_kernel(x_ref, w_ref, b_ref, cos_ref, sin_ref,
                    qa_ref, ka_ref, va_ref, qd_ref, kd_ref, vd_ref):
    tm = x_ref.shape[0]
    x = x_ref[...].astype(BF16)
    cos = cos_ref[...]
    sin = sin_ref[...]
    lane = lax.broadcasted_iota(jnp.int32, (tm, LANES), 1)
    first_half = (lane % HEAD_DIM) < (HEAD_DIM // 2)
    q_scale = HEAD_DIM ** -0.5 * LOG2E

    def proj(c0, c1):
        return jnp.dot(x, w_ref[:, c0:c1], preferred_element_type=F32) + b_ref[:, c0:c1]

    def rope(c):
        rot = jnp.where(first_half, pltpu.roll(c, LANES - 32, 1), pltpu.roll(c, 32, 1))
        return c * cos + rot * sin

    def emit(out_ref, c0, c1, rotary, scale):
        for g0 in range(c0, c1, 256):
            g1 = min(g0 + 256, c1)
            p = proj(g0, g1)
            for j in range(0, g1 - g0, LANES):
                c = p[:, j:j + LANES]
                if rotary:
                    c = rope(c)
                if scale != 1.0:
                    c = c * scale
                out_ref[:, g0 - c0 + j:g0 - c0 + j + LANES] = c.astype(out_ref.dtype)

    emit(qa_ref, _AQ, _AK, True, q_scale)
    emit(ka_ref, _AK, _AV, True, 1.0)
    emit(va_ref, _AV, _BQ, False, 1.0)
    emit(qd_ref, _BQ, _BK, True, q_scale)
    emit(kd_ref, _BK, _BV, True, 1.0)
    emit(vd_ref, _BV, _END, False, 1.0)


def _in_proj(x2, w_in, b_in, cos128, sin128, tm=512):
    T = x2.shape[0]
    outs = [(T, 512), (T, 128), (T, 128), (T, 512), (T, 512), (T, 512)]
    return pl.pallas_call(
        _in_proj_kernel,
        grid=(T // tm,),
        in_specs=[
            pl.BlockSpec((tm, D_MODEL), lambda i: (i, 0)),
            pl.BlockSpec((D_MODEL, IN_WIDTH), lambda i: (0, 0)),
            pl.BlockSpec((1, IN_WIDTH), lambda i: (0, 0)),
            pl.BlockSpec((tm, LANES), lambda i: (i, 0)),
            pl.BlockSpec((tm, LANES), lambda i: (i, 0)),
        ],
        out_specs=[pl.BlockSpec((tm, s[1]), lambda i: (i, 0)) for s in outs],
        out_shape=[jax.ShapeDtypeStruct(s, BF16) for s in outs],
        compiler_params=_cparams(48),
        name="in_proj_rope",
    )(x2, w_in, b_in, cos128, sin128)


def _swa_kernel(sink_ref, q_ref, kp_ref, kc_ref, vp_ref, vc_ref, o_ref):
    i = pl.program_id(0)
    tq = q_ref.shape[0]
    back = SWA_WIN_CHUNKS * CHUNK
    keys = jnp.concatenate([kp_ref[tq - back:, :], kc_ref[...]], axis=0)
    vals = jnp.concatenate([vp_ref[tq - back:, :], vc_ref[...]], axis=0)
    nk = tq + back
    qc = lax.broadcasted_iota(jnp.int32, (tq, nk), 0) // CHUNK
    kc = lax.broadcasted_iota(jnp.int32, (tq, nk), 1) // CHUNK - SWA_WIN_CHUNKS
    lo = jnp.where(i > 0, qc - SWA_WIN_CHUNKS, jnp.maximum(qc - SWA_WIN_CHUNKS, 0))
    bias = jnp.where(kc <= qc, jnp.where(kc >= lo, 0.0, NEG_BIG), NEG_BIG).astype(F32)
    lane = lax.broadcasted_iota(jnp.int32, (tq, LANES), 1)
    low = lane < HEAD_DIM
    zero = jnp.zeros((), BF16)
    for c in range(SWA_Q_HEADS // 2):
        qch = q_ref[:, c * LANES:(c + 1) * LANES]
        outs = []
        for half in range(2):
            h = c + (SWA_Q_HEADS // 2) * half
            qm = jnp.where(low if half == 0 else jnp.logical_not(low), qch, zero)
            s = lax.dot_general(qm, keys, _NT, preferred_element_type=F32) + bias
            sink = sink_ref[h] * LOG2E
            m = jnp.maximum(jnp.max(s, axis=-1, keepdims=True), sink)
            p = jnp.exp2(s - m)
            denom = jnp.sum(p, axis=-1, keepdims=True) + jnp.exp2(sink - m)
            o = jnp.dot(p.astype(BF16), vals, preferred_element_type=F32)
            outs.append(o / denom)
        o_ref[:, c * LANES:(c + 1) * LANES] = jnp.where(low, outs[0], outs[1]).astype(o_ref.dtype)


def _swa(sinks, qa, ka, va, tq=256):
    T = qa.shape[0]
    prev = lambda i: (jnp.maximum(i - 1, 0), 0)
    cur = lambda i: (i, 0)
    return pl.pallas_call(
        _swa_kernel,
        grid=(T // tq,),
        in_specs=[
            pl.BlockSpec(memory_space=pltpu.SMEM),
            pl.BlockSpec((tq, 512), cur),
            pl.BlockSpec((tq, LANES), prev),
            pl.BlockSpec((tq, LANES), cur),
            pl.BlockSpec((tq, LANES), prev),
            pl.BlockSpec((tq, LANES), cur),
        ],
        out_specs=pl.BlockSpec((tq, 512), cur),
        out_shape=jax.ShapeDtypeStruct((T, 512), BF16),
        compiler_params=_cparams(32),
        name="swa_sink_attention",
    )(sinks, qa, ka, ka, va, va)


def _diff_kernel(lam_ref, q_ref, k_ref, v_ref, g_ref, o_ref, acc_ref, *, lam_init):
    i = pl.program_id(1)
    tq = q_ref.shape[0]
    tk = tq
    q = q_ref[...]
    lane = lax.broadcasted_iota(jnp.int32, (tq, LANES), 1)
    zero = jnp.zeros((), BF16)
    qs = (jnp.where(lane < HEAD_DIM, q, zero), jnp.where(lane >= HEAD_DIM, q, zero))
    ones_col = jnp.where(lax.broadcasted_iota(jnp.int32, (tk, LANES), 1) == 0, 1.0, 0.0).astype(BF16)
    acc_ref[...] = jnp.zeros_like(acc_ref)

    def step(j, carry, diagonal):
        start = pl.multiple_of(j * tk, tk)
        k = k_ref[pl.ds(start, tk), :]
        v = v_ref[pl.ds(start, tk), :]
        vext = jnp.concatenate([v, ones_col], axis=1)
        new = []
        for mi in range(2):
            s = lax.dot_general(qs[mi], k, _NT, preferred_element_type=F32)
            if diagonal:
                rc = lax.broadcasted_iota(jnp.int32, (tq, tk), 0) // CHUNK
                cc = lax.broadcasted_iota(jnp.int32, (tq, tk), 1) // CHUNK
                s = jnp.where(cc <= rc, s, NEG_BIG)
            m_old = carry[mi]
            m_new = jnp.maximum(m_old, jnp.max(s, axis=-1, keepdims=True))
            alpha = jnp.exp2(m_old - m_new)
            p = jnp.exp2(s - m_new).astype(BF16)
            acc_ref[mi] = acc_ref[mi] * alpha + jnp.dot(p, vext, preferred_element_type=F32)
            new.append(m_new)
        return tuple(new)

    init = (jnp.full((tq, 1), NEG_BIG, F32), jnp.full((tq, 1), NEG_BIG, F32))
    carry = lax.fori_loop(0, i, lambda j, c: step(j, c, False), init)
    step(i, carry, True)

    a0 = acc_ref[0]
    a1 = acc_ref[1]
    lam = lam_ref[0]
    o = a0[:, :DIFF_V_DIM] / a0[:, DIFF_V_DIM:DIFF_V_DIM + 1] - lam * (a1[:, :DIFF_V_DIM] / a1[:, DIFF_V_DIM:DIFF_V_DIM + 1])
    ms = jnp.mean(o * o, axis=-1, keepdims=True)
    o = o * lax.rsqrt(ms + NORM_EPS) * g_ref[...] * (1.0 - lam_init)
    o_ref[...] = o.astype(o_ref.dtype)


def _diff_attn(lam, qd, kd, vd, sub_g, lam_init, tq=512):
    T = qd.shape[0]
    return pl.pallas_call(
        functools.partial(_diff_kernel, lam_init=lam_init),
        grid=(DIFF_HEADS, T // tq),
        in_specs=[
            pl.BlockSpec(memory_space=pltpu.SMEM),
            pl.BlockSpec((tq, LANES), lambda h, i: (i, h)),
            pl.BlockSpec((T, LANES), lambda h, i: (0, h)),
            pl.BlockSpec((T, LANES), lambda h, i: (0, h)),
            pl.BlockSpec((1, LANES), lambda h, i: (0, 0)),
        ],
        out_specs=pl.BlockSpec((tq, LANES), lambda h, i: (i, h)),
        out_shape=jax.ShapeDtypeStruct((T, DIFF_HEADS * DIFF_V_DIM), BF16),
        scratch_shapes=[pltpu.VMEM((2, tq, 2 * LANES), F32)],
        compiler_params=_cparams(48, 2),
        name="diff_attention",
    )(lam, qd, kd, vd, sub_g)


def _layer_norm(y, g, b):
    mu = jnp.mean(y, axis=-1, keepdims=True)
    d = y - mu
    var = jnp.mean(d * d, axis=-1, keepdims=True)
    return d * lax.rsqrt(var + NORM_EPS) * g + b


def _out_proj_kernel(oa_ref, ob_ref, x_ref, wo_ref, g_ref, b_ref, wrh_ref, wrl_ref, x1_ref, logit_ref):
    half = oa_ref.shape[1]
    mix = jnp.dot(oa_ref[...], wo_ref[:half, :], preferred_element_type=F32)
    mix = mix + jnp.dot(ob_ref[...], wo_ref[half:, :], preferred_element_type=F32)
    x1 = _layer_norm(DEEPNORM_ALPHA * x_ref[...] + mix, g_ref[...], b_ref[...])
    x1_ref[...] = x1
    xh = x1.astype(BF16)
    xl = (x1 - xh.astype(F32)).astype(BF16)
    wh = wrh_ref[...]
    logits = jnp.dot(xh, wh, preferred_element_type=F32)
    logits = logits + jnp.dot(xl, wh, preferred_element_type=F32)
    logits = logits + jnp.dot(xh, wrl_ref[...], preferred_element_type=F32)
    logit_ref[...] = logits


def _out_proj(oa, ob, x2, w_out, g, b, wr_hi, wr_lo, tm=512):
    T = x2.shape[0]
    row = lambda i: (i, 0)
    fix = lambda i: (0, 0)
    return pl.pallas_call(
        _out_proj_kernel,
        grid=(T // tm,),
        in_specs=[
            pl.BlockSpec((tm, 512), row),
            pl.BlockSpec((tm, 512), row),
            pl.BlockSpec((tm, D_MODEL), row),
            pl.BlockSpec((D_MODEL, D_MODEL), fix),
            pl.BlockSpec((1, D_MODEL), fix),
            pl.BlockSpec((1, D_MODEL), fix),
            pl.BlockSpec((D_MODEL, N_EXPERTS), fix),
            pl.BlockSpec((D_MODEL, N_EXPERTS), fix),
        ],
        out_specs=[pl.BlockSpec((tm, D_MODEL), row), pl.BlockSpec((tm, N_EXPERTS), row)],
        out_shape=[jax.ShapeDtypeStruct((T, D_MODEL), F32), jax.ShapeDtypeStruct((T, N_EXPERTS), F32)],
        compiler_params=_cparams(48),
        name="out_proj_ln1_router",
    )(oa, ob, x2, w_out, g, b, wr_hi, wr_lo)


def _route_kernel(logit_ref, bias_ref, idx_ref, gate_ref):
    tm = logit_ref.shape[0]
    scores = 1.0 / (1.0 + jnp.exp(-logit_ref[...]))
    sel = scores + bias_ref[...]
    lane = lax.broadcasted_iota(jnp.int32, (tm, N_EXPERTS), 1)
    lanef = lane.astype(F32)
    grp = lane // GROUP_SIZE
    neg_inf = -jnp.inf

    def first_argmax(v):
        m = jnp.max(v, axis=-1, keepdims=True)
        idx = jnp.min(jnp.where(v == m, lanef, float(N_EXPERTS)), axis=-1, keepdims=True)
        return m, idx

    gscore = []
    for g in range(N_GROUPS):
        vg = jnp.where(grp == g, sel, neg_inf)
        m1, i1 = first_argmax(vg)
        m2 = jnp.max(jnp.where(lanef == i1, neg_inf, vg), axis=-1, keepdims=True)
        gscore.append(m1 + m2)
    gmask = jnp.zeros((tm, N_EXPERTS), jnp.bool_)
    keep = []
    for g in range(N_GROUPS):
        rank = jnp.zeros((tm, 1), F32)
        for o in range(N_GROUPS):
            if o == g:
                continue
            ahead = (gscore[o] > gscore[g]) if o > g else (gscore[o] >= gscore[g])
            rank = rank + jnp.where(ahead, 1.0, 0.0)
        keep.append(jnp.where(rank < float(TOPK_GROUPS), 1.0, 0.0))
    keepf = jnp.zeros((tm, N_EXPERTS), F32)
    for g in range(N_GROUPS):
        keepf = jnp.where(grp == g, keep[g], keepf)
    cand = jnp.where(keepf > 0.5, sel, neg_inf)

    out_lane = lax.broadcasted_iota(jnp.int32, (tm, LANES), 1)
    idx_acc = jnp.zeros((tm, LANES), F32)
    sc_acc = jnp.zeros((tm, LANES), F32)
    for k in range(TOP_K):
        _, ik = first_argmax(cand)
        hit = lanef == ik
        sk = jnp.sum(jnp.where(hit, scores, 0.0), axis=-1, keepdims=True)
        idx_acc = jnp.where(out_lane == k, ik, idx_acc)
        sc_acc = jnp.where(out_lane == k, sk, sc_acc)
        cand = jnp.where(hit, neg_inf, cand)
    total = jnp.sum(sc_acc, axis=-1, keepdims=True)
    idx_ref[...] = idx_acc.astype(jnp.int32)
    gate_ref[...] = sc_acc / total * ROUTED_SCALE


def _route(logits, r_bias, tm=512):
    T = logits.shape[0]
    row = lambda i: (i, 0)
    return pl.pallas_call(
        _route_kernel,
        grid=(T // tm,),
        in_specs=[pl.BlockSpec((tm, N_EXPERTS), row), pl.BlockSpec((1, N_EXPERTS), lambda i: (0, 0))],
        out_specs=[pl.BlockSpec((tm, LANES), row), pl.BlockSpec((tm, LANES), row)],
        out_shape=[jax.ShapeDtypeStruct((T, LANES), jnp.int32), jax.ShapeDtypeStruct((T, LANES), F32)],
        compiler_params=_cparams(32),
        name="moe_route",
    )(logits, r_bias)


def _start_row_gather(idx_ref, n_rows, src_hbm, dst, row0, sem):
    def body(r, carry):
        t = idx_ref[0, 0, r]
        pltpu.make_async_copy(src_hbm.at[pl.ds(t, 1)], dst.at[pl.ds(row0 + r, 1)], sem).start()
        return carry
    lax.fori_loop(0, n_rows, body, 0, unroll=8)


def _wait_row_gather(n_rows, src_hbm, dst, row0, sem):
    pltpu.make_async_copy(src_hbm.at[pl.ds(0, n_rows)], dst.at[pl.ds(row0, n_rows)], sem).wait()


def _expert_kernel(be_ref, tokc_ref, tokn_ref, g_ref, x_hbm, wg_ref, wu_ref, wd_ref, y_ref, buf, sem):
    del be_ref
    b = pl.program_id(0)
    nb = pl.num_programs(0)
    rows = MOE_BLOCK
    slot = b % 2

    @pl.when(b == 0)
    def _():
        _start_row_gather(tokc_ref, rows, x_hbm, buf, 0, sem.at[0])

    @pl.when(b + 1 < nb)
    def _():
        nslot = (b + 1) % 2
        _start_row_gather(tokn_ref, rows, x_hbm, buf, nslot * rows, sem.at[nslot])

    row0 = pl.multiple_of(slot * rows, rows)
    _wait_row_gather(rows, x_hbm, buf, row0, sem.at[slot])
    xg = buf[pl.ds(row0, rows), :].astype(BF16)
    a = jnp.dot(xg, wg_ref[...].astype(BF16), preferred_element_type=F32)
    u = jnp.dot(xg, wu_ref[...].astype(BF16), preferred_element_type=F32)
    h = (a / (1.0 + jnp.exp(-a)) * u).astype(BF16)
    y = jnp.dot(h, wd_ref[...].astype(BF16), preferred_element_type=F32)
    y_ref[...] = y * g_ref[0]


def _experts(block_e, tok_blocks, g_blocks, x1, w_gate, w_up, w_down):
    NB = block_e.shape[0]
    P = NB * MOE_BLOCK
    nxt = lambda b, be: (jnp.minimum(b + 1, NB - 1), 0, 0)
    cur = lambda b, be: (b, 0, 0)
    wsel = lambda b, be: (be[b], 0, 0)
    grid_spec = pltpu.PrefetchScalarGridSpec(
        num_scalar_prefetch=1,
        grid=(NB,),
        in_specs=[
            pl.BlockSpec((1, 1, MOE_BLOCK), cur, memory_space=pltpu.SMEM),
            pl.BlockSpec((1, 1, MOE_BLOCK), nxt, memory_space=pltpu.SMEM),
            pl.BlockSpec((1, MOE_BLOCK, 1), cur),
            pl.BlockSpec(memory_space=pl.ANY),
            pl.BlockSpec((None, D_MODEL, EXPERT_FF), wsel),
            pl.BlockSpec((None, D_MODEL, EXPERT_FF), wsel),
            pl.BlockSpec((None, EXPERT_FF, D_MODEL), wsel),
        ],
        out_specs=pl.BlockSpec((MOE_BLOCK, D_MODEL), lambda b, be: (b, 0)),
        scratch_shapes=[pltpu.VMEM((2 * MOE_BLOCK, D_MODEL), F32), pltpu.SemaphoreType.DMA((2,))],
    )
    return pl.pallas_call(
        _expert_kernel,
        grid_spec=grid_spec,
        out_shape=jax.ShapeDtypeStruct((P, D_MODEL), F32),
        compiler_params=_cparams(48),
        name="moe_experts",
    )(block_e, tok_blocks, tok_blocks, g_blocks, x1, w_gate, w_up, w_down)


def _combine_kernel(posc_ref, posn_ref, ys_hbm, x1_ref, wsg_ref, wsu_ref, wsd_ref, g_ref, b_ref, out_ref, buf, sem):
    i = pl.program_id(0)
    n = pl.num_programs(0)
    tt = x1_ref.shape[0]
    rows = TOP_K * tt
    slot = i % 2

    @pl.when(i == 0)
    def _():
        _start_row_gather(posc_ref, rows, ys_hbm, buf, 0, sem.at[0])

    @pl.when(i + 1 < n)
    def _():
        nslot = (i + 1) % 2
        _start_row_gather(posn_ref, rows, ys_hbm, buf, nslot * rows, sem.at[nslot])

    x1 = x1_ref[...]
    xb = x1.astype(BF16)
    a = jnp.dot(xb, wsg_ref[...], preferred_element_type=F32)
    u = jnp.dot(xb, wsu_ref[...], preferred_element_type=F32)
    h = (a / (1.0 + jnp.exp(-a)) * u).astype(BF16)
    ffn = jnp.dot(h, wsd_ref[...], preferred_element_type=F32)

    row0 = pl.multiple_of(slot * rows, rows)
    _wait_row_gather(rows, ys_hbm, buf, row0, sem.at[slot])
    for k in range(TOP_K):
        ffn = ffn + buf[pl.ds(row0 + k * tt, tt), :]
    out_ref[...] = _layer_norm(DEEPNORM_ALPHA * x1 + ffn, g_ref[...], b_ref[...])


def _combine(pos_blocks, ys, x1, ws_gate, ws_up, ws_down, g, b, tt=128):
    T = x1.shape[0]
    n = T // tt
    rows = TOP_K * tt
    row = lambda i: (i, 0)
    fix = lambda i: (0, 0)
    return pl.pallas_call(
        _combine_kernel,
        grid=(n,),
        in_specs=[
            pl.BlockSpec((1, 1, rows), lambda i: (i, 0, 0), memory_space=pltpu.SMEM),
            pl.BlockSpec((1, 1, rows), lambda i: (jnp.minimum(i + 1, n - 1), 0, 0), memory_space=pltpu.SMEM),
            pl.BlockSpec(memory_space=pl.ANY),
            pl.BlockSpec((tt, D_MODEL), row),
            pl.BlockSpec((D_MODEL, SHARED_FF), fix),
            pl.BlockSpec((D_MODEL, SHARED_FF), fix),
            pl.BlockSpec((SHARED_FF, D_MODEL), fix),
            pl.BlockSpec((1, D_MODEL), fix),
            pl.BlockSpec((1, D_MODEL), fix),
        ],
        out_specs=pl.BlockSpec((tt, D_MODEL), row),
        out_shape=jax.ShapeDtypeStruct((T, D_MODEL), F32),
        scratch_shapes=[pltpu.VMEM((2 * rows, D_MODEL), F32), pltpu.SemaphoreType.DMA((2,))],
        compiler_params=_cparams(48),
        name="moe_combine_ln2",
    )(pos_blocks, pos_blocks, ys, x1, ws_gate, ws_up, ws_down, g, b)


def _rope_tables(T):
    inv = 1.0 / (ROPE_THETA ** (jnp.arange(0, HEAD_DIM, 2, dtype=F32) / HEAD_DIM))
    ang = jnp.arange(T, dtype=F32)[:, None] * inv[None, :]
    cos, sin = jnp.cos(ang), jnp.sin(ang)
    cos128 = jnp.tile(cos, (1, 4))
    sin128 = jnp.tile(jnp.concatenate([-sin, sin], axis=1), (1, 2))
    return cos128, sin128


def _swa_head_perm():
    half = SWA_Q_HEADS // 2
    return np.array([(c + half * hh) * HEAD_DIM + d for c in range(half) for hh in range(2) for d in range(HEAD_DIM)])


def _dispatch_tables(top_e, gate, T):
    A = T * TOP_K
    NB = (A + N_EXPERTS * (MOE_BLOCK - 1) + MOE_BLOCK - 1) // MOE_BLOCK
    P = NB * MOE_BLOCK
    e_flat = top_e.reshape(A)
    order = jnp.argsort(e_flat)
    e_s = e_flat[order]
    tok_s = (order // TOP_K).astype(jnp.int32)
    g_s = gate.reshape(A)[order]
    counts = jnp.bincount(e_flat, length=N_EXPERTS)
    padded = (counts + MOE_BLOCK - 1) // MOE_BLOCK * MOE_BLOCK
    starts = jnp.cumsum(counts) - counts
    pends = jnp.cumsum(padded)
    pstarts = pends - padded
    dest = (pstarts[e_s] + (jnp.arange(A) - starts[e_s])).astype(jnp.int32)
    tok_buf = jnp.zeros((P,), jnp.int32).at[dest].set(tok_s)
    g_buf = jnp.zeros((P,), F32).at[dest].set(g_s)
    pos = jnp.zeros((A,), jnp.int32).at[order].set(dest)
    block_e = jnp.minimum(jnp.searchsorted(pends, jnp.arange(NB) * MOE_BLOCK, side='right'),
                          N_EXPERTS - 1).astype(jnp.int32)
    return NB, block_e, tok_buf, g_buf, pos


def kernel(x, w_in, b_in, sinks, lam_q1, lam_k1, lam_q2, lam_k2, diff_norm_g, w_out, ln1_g, ln1_b,
           w_router, router_bias, w_gate, w_up, w_down, ws_gate, ws_up, ws_down, ln2_g, ln2_b):
    B, S, D = x.shape
    assert D == D_MODEL and w_in.shape[0] == DEPTH
    T = B * S
    assert B == 1 and T % 512 == 0
    cos128, sin128 = _rope_tables(S)
    perm = _swa_head_perm()
    x2 = x.reshape(T, D)
    for l in range(DEPTH):
        lam_init = 0.8 - 0.6 * math.exp(-0.3 * l)
        wi = w_in[l]
        wi = jnp.concatenate([wi[:, perm], wi[:, _AK:]], axis=1).astype(BF16)
        bi = jnp.concatenate([b_in[l][perm], b_in[l][_AK:]])[None, :]
        wo = jnp.concatenate([w_out[l][perm], w_out[l][512:]], axis=0).astype(BF16)
        sink_p = sinks[l].astype(F32)
        lam = (jnp.exp(jnp.sum(lam_q1[l].astype(F32) * lam_k1[l].astype(F32)))
               - jnp.exp(jnp.sum(lam_q2[l].astype(F32) * lam_k2[l].astype(F32))) + lam_init).reshape(1)
        wr = w_router[l]
        wr_hi = wr.astype(BF16)
        wr_lo = (wr - wr_hi.astype(F32)).astype(BF16)

        qa, ka, va, qd, kd, vd = _in_proj(x2, wi, bi, cos128, sin128)
        oa = _swa(sink_p, qa, ka, va)
        ob = _diff_attn(lam, qd, kd, vd, diff_norm_g[l][None, :], lam_init)
        x1, logits = _out_proj(oa, ob, x2, wo, ln1_g[l][None, :], ln1_b[l][None, :], wr_hi, wr_lo)
        idx, gates = _route(logits, router_bias[l][None, :])
        top_e = idx[:, :TOP_K]
        gate = gates[:, :TOP_K]

        NB, block_e, tok_buf, g_buf, pos = _dispatch_tables(top_e, gate, T)
        ys = _experts(block_e, tok_buf.reshape(NB, 1, MOE_BLOCK), g_buf.reshape(NB, MOE_BLOCK, 1),
                      x1, w_gate[l], w_up[l], w_down[l])
        tt = 128
        pos_blocks = pos.reshape(T // tt, tt, TOP_K).transpose(0, 2, 1).reshape(T // tt, 1, TOP_K * tt)
        x2 = _combine(pos_blocks, ys, x1, ws_gate[l].astype(BF16), ws_up[l].astype(BF16),
                      ws_down[l].astype(BF16), ln2_g[l][None, :], ln2_b[l][None, :], tt=tt)
    return x2.reshape(B, S, D)
```

```python
import functools
import math

import numpy as np
import jax
import jax.numpy as jnp
from jax import lax
from jax.experimental import pallas as pl
from jax.experimental.pallas import tpu as pltpu

D_MODEL = 1024
CHUNK = 64
HEAD_DIM = 64
ROPE_THETA = 10000.0
SWA_Q_HEADS = 8
SWA_KV_HEADS = 2
SWA_WIN_CHUNKS = 2
DIFF_HEADS = 4
DIFF_V_DIM = 2 * HEAD_DIM
N_EXPERTS = 256
TOP_K = 8
N_GROUPS = 8
TOPK_GROUPS = 4
GROUP_SIZE = N_EXPERTS // N_GROUPS
EXPERT_FF = 256
SHARED_FF = 256
ROUTED_SCALE = 2.5
MOE_BLOCK = 128
DEPTH = 1
DEEPNORM_ALPHA = (2.0 * DEPTH) ** 0.25
NORM_EPS = 1e-5
LOG2E = math.log2(math.e)
NEG_BIG = -1e30
LANES = 128

_AQ, _AK, _AV, _BQ, _BK, _BV, _END = 0, 512, 640, 768, 1280, 1792, 2304
IN_WIDTH = _END

_NT = (((1,), (1,)), ((), ()))

F32 = jnp.float32
BF16 = jnp.bfloat16


def _cparams(vmem_mb, n_axes=1):
    return pltpu.CompilerParams(dimension_semantics=("arbitrary",) * n_axes,
                                vmem_limit_bytes=vmem_mb * 1024 * 1024)


def _in_proj_kernel(x_ref, w_ref, b_ref, cos_ref, sin_ref,
                    qa_ref, ka_ref, va_ref, qd_ref, kd_ref, vd_ref):
    tm = x_ref.shape[0]
    x = x_ref[...].astype(BF16)
    cos = cos_ref[...]
    sin = sin_ref[...]
    lane = lax.broadcasted_iota(jnp.int32, (tm, LANES), 1)
    first_half = (lane % HEAD_DIM) < (HEAD_DIM // 2)
    q_scale = HEAD_DIM ** -0.5 * LOG2E

    def proj(c0, c1):
        return jnp.dot(x, w_ref[:, c0:c1], preferred_element_type=F32) + b_ref[:, c0:c1]

    def rope(c):
        rot = jnp.where(first_half, pltpu.roll(c, LANES - 32, 1), pltpu.roll(c, 32, 1))
        return c * cos + rot * sin

    def emit(out_ref, c0, c1, rotary, scale):
        for g0 in range(c0, c1, 256):
            g1 = min(g0 + 256, c1)
            p = proj(g0, g1)
            for j in range(0, g1 - g0, LANES):
                c = p[:, j:j + LANES]
                if rotary:
                    c = rope(c)
                if scale != 1.0:
                    c = c * scale
                out_ref[:, g0 - c0 + j:g0 - c0 + j + LANES] = c.astype(out_ref.dtype)

    emit(qa_ref, _AQ, _AK, True, q_scale)
    emit(ka_ref, _AK, _AV, True, 1.0)
    emit(va_ref, _AV, _BQ, False, 1.0)
    emit(qd_ref, _BQ, _BK, True, q_scale)
    emit(kd_ref, _BK, _BV, True, 1.0)
    emit(vd_ref, _BV, _END, False, 1.0)


def _in_proj(x2, w_in, b_in, cos128, sin128, tm=512):
    T = x2.shape[0]
    outs = [(T, 512), (T, 128), (T, 128), (T, 512), (T, 512), (T, 512)]
    return pl.pallas_call(
        _in_proj_kernel,
        grid=(T // tm,),
        in_specs=[
            pl.BlockSpec((tm, D_MODEL), lambda i: (i, 0)),
            pl.BlockSpec((D_MODEL, IN_WIDTH), lambda i: (0, 0)),
            pl.BlockSpec((1, IN_WIDTH), lambda i: (0, 0)),
            pl.BlockSpec((tm, LANES), lambda i: (i, 0)),
            pl.BlockSpec((tm, LANES), lambda i: (i, 0)),
        ],
        out_specs=[pl.BlockSpec((tm, s[1]), lambda i: (i, 0)) for s in outs],
        out_shape=[jax.ShapeDtypeStruct(s, BF16) for s in outs],
        compiler_params=_cparams(48),
        name="in_proj_rope",
    )(x2, w_in, b_in, cos128, sin128)


def _swa_kernel(sink_ref, q_ref, kp_ref, kc_ref, vp_ref, vc_ref, o_ref):
    i = pl.program_id(0)
    tq = q_ref.shape[0]
    back = SWA_WIN_CHUNKS * CHUNK
    keys = jnp.concatenate([kp_ref[tq - back:, :], kc_ref[...]], axis=0)
    vals = jnp.concatenate([vp_ref[tq - back:, :], vc_ref[...]], axis=0)
    nk = tq + back
    qc = lax.broadcasted_iota(jnp.int32, (tq, nk), 0) // CHUNK
    kc = lax.broadcasted_iota(jnp.int32, (tq, nk), 1) // CHUNK - SWA_WIN_CHUNKS
    lo = jnp.where(i > 0, qc - SWA_WIN_CHUNKS, jnp.maximum(qc - SWA_WIN_CHUNKS, 0))
    bias = jnp.where(kc <= qc, jnp.where(kc >= lo, 0.0, NEG_BIG), NEG_BIG).astype(F32)
    lane = lax.broadcasted_iota(jnp.int32, (tq, LANES), 1)
    low = lane < HEAD_DIM
    zero = jnp.zeros((), BF16)
    for c in range(SWA_Q_HEADS // 2):
        qch = q_ref[:, c * LANES:(c + 1) * LANES]
        outs = []
        for half in range(2):
            h = c + (SWA_Q_HEADS // 2) * half
            qm = jnp.where(low if half == 0 else jnp.logical_not(low), qch, zero)
            s = lax.dot_general(qm, keys, _NT, preferred_element_type=F32) + bias
            sink = sink_ref[h] * LOG2E
            m = jnp.maximum(jnp.max(s, axis=-1, keepdims=True), sink)
            p = jnp.exp2(s - m)
            denom = jnp.sum(p, axis=-1, keepdims=True) + jnp.exp2(sink - m)
            o = jnp.dot(p.astype(BF16), vals, preferred_element_type=F32)
            outs.append(o / denom)
        o_ref[:, c * LANES:(c + 1) * LANES] = jnp.where(low, outs[0], outs[1]).astype(o_ref.dtype)


def _swa(sinks, qa, ka, va, tq=256):
    T = qa.shape[0]
    prev = lambda i: (jnp.maximum(i - 1, 0), 0)
    cur = lambda i: (i, 0)
    return pl.pallas_call(
        _swa_kernel,
        grid=(T // tq,),
        in_specs=[
            pl.BlockSpec(memory_space=pltpu.SMEM),
            pl.BlockSpec((tq, 512), cur),
            pl.BlockSpec((tq, LANES), prev),
            pl.BlockSpec((tq, LANES), cur),
            pl.BlockSpec((tq, LANES), prev),
            pl.BlockSpec((tq, LANES), cur),
        ],
        out_specs=pl.BlockSpec((tq, 512), cur),
        out_shape=jax.ShapeDtypeStruct((T, 512), BF16),
        compiler_params=_cparams(32),
        name="swa_sink_attention",
    )(sinks, qa, ka, ka, va, va)


def _diff_kernel(lam_ref, q_ref, k_ref, v_ref, g_ref, o_ref, acc_ref, *, lam_init):
    i = pl.program_id(1)
    tq = q_ref.shape[0]
    tk = tq
    q = q_ref[...]
    lane = lax.broadcasted_iota(jnp.int32, (tq, LANES), 1)
    zero = jnp.zeros((), BF16)
    qs = (jnp.where(lane < HEAD_DIM, q, zero), jnp.where(lane >= HEAD_DIM, q, zero))
    ones_col = jnp.where(lax.broadcasted_iota(jnp.int32, (tk, LANES), 1) == 0, 1.0, 0.0).astype(BF16)
    acc_ref[...] = jnp.zeros_like(acc_ref)

    def step(j, carry, diagonal):
        start = pl.multiple_of(j * tk, tk)
        k = k_ref[pl.ds(start, tk), :]
        v = v_ref[pl.ds(start, tk), :]
        vext = jnp.concatenate([v, ones_col], axis=1)
        new = []
        for mi in range(2):
            s = lax.dot_general(qs[mi], k, _NT, preferred_element_type=F32)
            if diagonal:
                rc = lax.broadcasted_iota(jnp.int32, (tq, tk), 0) // CHUNK
                cc = lax.broadcasted_iota(jnp.int32, (tq, tk), 1) // CHUNK
                s = jnp.where(cc <= rc, s, NEG_BIG)
            m_old = carry[mi]
            m_new = jnp.maximum(m_old, jnp.max(s, axis=-1, keepdims=True))
            alpha = jnp.exp2(m_old - m_new)
            p = jnp.exp2(s - m_new).astype(BF16)
            acc_ref[mi] = acc_ref[mi] * alpha + jnp.dot(p, vext, preferred_element_type=F32)
            new.append(m_new)
        return tuple(new)

    init = (jnp.full((tq, 1), NEG_BIG, F32), jnp.full((tq, 1), NEG_BIG, F32))
    carry = lax.fori_loop(0, i, lambda j, c: step(j, c, False), init)
    step(i, carry, True)

    a0 = acc_ref[0]
    a1 = acc_ref[1]
    lam = lam_ref[0]
    o = a0[:, :DIFF_V_DIM] / a0[:, DIFF_V_DIM:DIFF_V_DIM + 1] - lam * (a1[:, :DIFF_V_DIM] / a1[:, DIFF_V_DIM:DIFF_V_DIM + 1])
    ms = jnp.mean(o * o, axis=-1, keepdims=True)
    o = o * lax.rsqrt(ms + NORM_EPS) * g_ref[...] * (1.0 - lam_init)
    o_ref[...] = o.astype(o_ref.dtype)


def _diff_attn(lam, qd, kd, vd, sub_g, lam_init, tq=512):
    T = qd.shape[0]
    return pl.pallas_call(
        functools.partial(_diff_kernel, lam_init=lam_init),
        grid=(DIFF_HEADS, T // tq),
        in_specs=[
            pl.BlockSpec(memory_space=pltpu.SMEM),
            pl.BlockSpec((tq, LANES), lambda h, i: (i, h)),
            pl.BlockSpec((T, LANES), lambda h, i: (0, h)),
            pl.BlockSpec((T, LANES), lambda h, i: (0, h)),
            pl.BlockSpec((1, LANES), lambda h, i: (0, 0)),
        ],
        out_specs=pl.BlockSpec((tq, LANES), lambda h, i: (i, h)),
        out_shape=jax.ShapeDtypeStruct((T, DIFF_HEADS * DIFF_V_DIM), BF16),
        scratch_shapes=[pltpu.VMEM((2, tq, 2 * LANES), F32)],
        compiler_params=_cparams(48, 2),
        name="diff_attention",
    )(lam, qd, kd, vd, sub_g)


def _layer_norm(y, g, b):
    mu = jnp.mean(y, axis=-1, keepdims=True)
    d = y - mu
    var = jnp.mean(d * d, axis=-1, keepdims=True)
    return d * lax.rsqrt(var + NORM_EPS) * g + b


def _out_proj_kernel(oa_ref, ob_ref, x_ref, wo_ref, g_ref, b_ref, wrh_ref, wrl_ref, x1_ref, logit_ref):
    half = oa_ref.shape[1]
    mix = jnp.dot(oa_ref[...], wo_ref[:half, :], preferred_element_type=F32)
    mix = mix + jnp.dot(ob_ref[...], wo_ref[half:, :], preferred_element_type=F32)
    x1 = _layer_norm(DEEPNORM_ALPHA * x_ref[...] + mix, g_ref[...], b_ref[...])
    x1_ref[...] = x1
    xh = x1.astype(BF16)
    xl = (x1 - xh.astype(F32)).astype(BF16)
    wh = wrh_ref[...]
    logits = jnp.dot(xh, wh, preferred_element_type=F32)
    logits = logits + jnp.dot(xl, wh, preferred_element_type=F32)
    logits = logits + jnp.dot(xh, wrl_ref[...], preferred_element_type=F32)
    logit_ref[...] = logits


def _out_proj(oa, ob, x2, w_out, g, b, wr_hi, wr_lo, tm=512):
    T = x2.shape[0]
    row = lambda i: (i, 0)
    fix = lambda i: (0, 0)
    return pl.pallas_call(
        _out_proj_kernel,
        grid=(T // tm,),
        in_specs=[
            pl.BlockSpec((tm, 512), row),
            pl.BlockSpec((tm, 512), row),
            pl.BlockSpec((tm, D_MODEL), row),
            pl.BlockSpec((D_MODEL, D_MODEL), fix),
            pl.BlockSpec((1, D_MODEL), fix),
            pl.BlockSpec((1, D_MODEL), fix),
            pl.BlockSpec((D_MODEL, N_EXPERTS), fix),
            pl.BlockSpec((D_MODEL, N_EXPERTS), fix),
        ],
        out_specs=[pl.BlockSpec((tm, D_MODEL), row), pl.BlockSpec((tm, N_EXPERTS), row)],
        out_shape=[jax.ShapeDtypeStruct((T, D_MODEL), F32), jax.ShapeDtypeStruct((T, N_EXPERTS), F32)],
        compiler_params=_cparams(48),
        name="out_proj_ln1_router",
    )(oa, ob, x2, w_out, g, b, wr_hi, wr_lo)


def _route_kernel(logit_ref, bias_ref, idx_ref, gate_ref, rank_ref, count_ref, run_ref):
    i = pl.program_id(0)
    tm = logit_ref.shape[0]
    scores = 1.0 / (1.0 + jnp.exp(-logit_ref[...]))
    sel = scores + bias_ref[...]
    lane = lax.broadcasted_iota(jnp.int32, (tm, N_EXPERTS), 1)
    lanef = lane.astype(F32)
    grp = lane // GROUP_SIZE
    neg_inf = -jnp.inf

    @pl.when(i == 0)
    def _():
        run_ref[...] = jnp.zeros_like(run_ref)

    def first_argmax(v):
        m = jnp.max(v, axis=-1, keepdims=True)
        idx = jnp.min(jnp.where(v == m, lanef, float(N_EXPERTS)), axis=-1, keepdims=True)
        return m, idx

    gscore = []
    for g in range(N_GROUPS):
        vg = jnp.where(grp == g, sel, neg_inf)
        m1, i1 = first_argmax(vg)
        m2 = jnp.max(jnp.where(lanef == i1, neg_inf, vg), axis=-1, keepdims=True)
        gscore.append(m1 + m2)
    keepf = jnp.zeros((tm, N_EXPERTS), F32)
    for g in range(N_GROUPS):
        ahead_n = jnp.zeros((tm, 1), F32)
        for o in range(N_GROUPS):
            if o == g:
                continue
            ahead = (gscore[o] > gscore[g]) if o > g else (gscore[o] >= gscore[g])
            ahead_n = ahead_n + jnp.where(ahead, 1.0, 0.0)
        keepf = jnp.where(grp == g, jnp.where(ahead_n < float(TOPK_GROUPS), 1.0, 0.0), keepf)
    cand = jnp.where(keepf > 0.5, sel, neg_inf)

    out_lane = lax.broadcasted_iota(jnp.int32, (tm, LANES), 1)
    idx_acc = jnp.zeros((tm, LANES), F32)
    sc_acc = jnp.zeros((tm, LANES), F32)
    hits = []
    chosen = jnp.zeros((tm, N_EXPERTS), F32)
    for k in range(TOP_K):
        _, ik = first_argmax(cand)
        hit = lanef == ik
        hits.append(hit)
        chosen = jnp.where(hit, 1.0, chosen)
        sk = jnp.sum(jnp.where(hit, scores, 0.0), axis=-1, keepdims=True)
        idx_acc = jnp.where(out_lane == k, ik, idx_acc)
        sc_acc = jnp.where(out_lane == k, sk, sc_acc)
        cand = jnp.where(hit, neg_inf, cand)
    total = jnp.sum(sc_acc, axis=-1, keepdims=True)
    idx_ref[...] = idx_acc.astype(jnp.int32)
    gate_ref[...] = sc_acc / total * ROUTED_SCALE

    r = lax.broadcasted_iota(jnp.int32, (tm, tm), 0)
    c = lax.broadcasted_iota(jnp.int32, (tm, tm), 1)
    lower = jnp.where(c < r, 1.0, 0.0).astype(BF16)
    before = jnp.dot(lower, chosen.astype(BF16), preferred_element_type=F32) + run_ref[...]
    rank_acc = jnp.zeros((tm, LANES), F32)
    for k in range(TOP_K):
        rk = jnp.sum(jnp.where(hits[k], before, 0.0), axis=-1, keepdims=True)
        rank_acc = jnp.where(out_lane == k, rk, rank_acc)
    rank_ref[...] = rank_acc.astype(jnp.int32)
    run_ref[...] = run_ref[...] + jnp.sum(chosen, axis=0, keepdims=True)
    count_ref[...] = run_ref[...].astype(jnp.int32)


def _route(logits, r_bias, tm=512):
    T = logits.shape[0]
    row = lambda i: (i, 0)
    fix = lambda i: (0, 0)
    return pl.pallas_call(
        _route_kernel,
        grid=(T // tm,),
        in_specs=[pl.BlockSpec((tm, N_EXPERTS), row), pl.BlockSpec((1, N_EXPERTS), fix)],
        out_specs=[pl.BlockSpec((tm, LANES), row), pl.BlockSpec((tm, LANES), row), pl.BlockSpec((tm, LANES), row),
                   pl.BlockSpec((1, N_EXPERTS), fix)],
        out_shape=[jax.ShapeDtypeStruct((T, LANES), jnp.int32), jax.ShapeDtypeStruct((T, LANES), F32),
                   jax.ShapeDtypeStruct((T, LANES), jnp.int32), jax.ShapeDtypeStruct((1, N_EXPERTS), jnp.int32)],
        scratch_shapes=[pltpu.VMEM((1, N_EXPERTS), F32)],
        compiler_params=_cparams(32),
        name="moe_route",
    )(logits, r_bias)


def _row_pos_kernel(idx_ref, rank_ref, pstart_ref, pos_ref):
    tm = idx_ref.shape[0]
    lanef = lax.broadcasted_iota(jnp.int32, (tm, N_EXPERTS), 1).astype(F32)
    out_lane = lax.broadcasted_iota(jnp.int32, (tm, LANES), 1)
    idx = idx_ref[...].astype(F32)
    pstart = pstart_ref[...]
    base = jnp.zeros((tm, LANES), F32)
    for k in range(TOP_K):
        ik = idx[:, k:k + 1]
        pk = jnp.sum(jnp.where(lanef == ik, pstart, 0.0), axis=-1, keepdims=True)
        base = jnp.where(out_lane == k, pk, base)
    pos_ref[...] = base.astype(jnp.int32) + rank_ref[...]


def _row_pos(idx, rank, pstarts, tm=512):
    T = idx.shape[0]
    row = lambda i: (i, 0)
    return pl.pallas_call(
        _row_pos_kernel,
        grid=(T // tm,),
        in_specs=[pl.BlockSpec((tm, LANES), row), pl.BlockSpec((tm, LANES), row),
                  pl.BlockSpec((1, N_EXPERTS), lambda i: (0, 0))],
        out_specs=pl.BlockSpec((tm, LANES), row),
        out_shape=jax.ShapeDtypeStruct((T, LANES), jnp.int32),
        compiler_params=_cparams(32),
        name="moe_row_pos",
    )(idx, rank, pstarts)


def _start_row_gather(idx_ref, n_rows, src_hbm, dst, row0, sem):
    def body(r, carry):
        t = idx_ref[0, 0, r]
        pltpu.make_async_copy(src_hbm.at[pl.ds(t, 1)], dst.at[pl.ds(row0 + r, 1)], sem).start()
        return carry
    lax.fori_loop(0, n_rows, body, 0, unroll=8)


def _wait_row_gather(n_rows, src_hbm, dst, row0, sem):
    pltpu.make_async_copy(src_hbm.at[pl.ds(0, n_rows)], dst.at[pl.ds(row0, n_rows)], sem).wait()


def _expert_kernel(be_ref, nused_ref, tokc_ref, tokn_ref, x_hbm, wg_ref, wu_ref, wd_ref, y_ref, buf, sem):
    del be_ref
    b = pl.program_id(0)
    nused = nused_ref[0]
    rows = MOE_BLOCK
    slot = b % 2

    @pl.when(b == 0)
    def _():
        _start_row_gather(tokc_ref, rows, x_hbm, buf, 0, sem.at[0])

    @pl.when(b + 1 < nused)
    def _():
        nslot = (b + 1) % 2
        _start_row_gather(tokn_ref, rows, x_hbm, buf, nslot * rows, sem.at[nslot])

    @pl.when(b < jnp.maximum(nused, 1))
    def _():
        row0 = pl.multiple_of(slot * rows, rows)
        _wait_row_gather(rows, x_hbm, buf, row0, sem.at[slot])
        xg = buf[pl.ds(row0, rows), :].astype(BF16)
        a = jnp.dot(xg, wg_ref[...].astype(BF16), preferred_element_type=F32)
        u = jnp.dot(xg, wu_ref[...].astype(BF16), preferred_element_type=F32)
        h = (a / (1.0 + jnp.exp(-a)) * u).astype(BF16)
        y_ref[...] = jnp.dot(h, wd_ref[...].astype(BF16), preferred_element_type=F32)

    @pl.when(b >= jnp.maximum(nused, 1))
    def _():
        y_ref[...] = jnp.zeros_like(y_ref)


def _experts(block_e, n_used, tok_blocks, x1, w_gate, w_up, w_down):
    NB = block_e.shape[0]
    P = NB * MOE_BLOCK
    nxt = lambda b, be, nu: (jnp.minimum(b + 1, NB - 1), 0, 0)
    cur = lambda b, be, nu: (b, 0, 0)
    wsel = lambda b, be, nu: (be[b], 0, 0)
    grid_spec = pltpu.PrefetchScalarGridSpec(
        num_scalar_prefetch=2,
        grid=(NB,),
        in_specs=[
            pl.BlockSpec((1, 1, MOE_BLOCK), cur, memory_space=pltpu.SMEM),
            pl.BlockSpec((1, 1, MOE_BLOCK), nxt, memory_space=pltpu.SMEM),
            pl.BlockSpec(memory_space=pl.ANY),
            pl.BlockSpec((None, D_MODEL, EXPERT_FF), wsel),
            pl.BlockSpec((None, D_MODEL, EXPERT_FF), wsel),
            pl.BlockSpec((None, EXPERT_FF, D_MODEL), wsel),
        ],
        out_specs=pl.BlockSpec((MOE_BLOCK, D_MODEL), lambda b, be, nu: (b, 0)),
        scratch_shapes=[pltpu.VMEM((2 * MOE_BLOCK, D_MODEL), F32), pltpu.SemaphoreType.DMA((2,))],
    )
    return pl.pallas_call(
        _expert_kernel,
        grid_spec=grid_spec,
        out_shape=jax.ShapeDtypeStruct((P, D_MODEL), F32),
        compiler_params=_cparams(48),
        name="moe_experts",
    )(block_e, n_used, tok_blocks, tok_blocks, x1, w_gate, w_up, w_down)


def _combine_kernel(posc_ref, posn_ref, ys_hbm, gate_ref, x1_ref, wsg_ref, wsu_ref, wsd_ref, g_ref, b_ref,
                    out_ref, buf, sem):
    i = pl.program_id(0)
    n = pl.num_programs(0)
    tt = x1_ref.shape[0]
    rows = TOP_K * tt
    slot = i % 2

    @pl.when(i == 0)
    def _():
        _start_row_gather(posc_ref, rows, ys_hbm, buf, 0, sem.at[0])

    @pl.when(i + 1 < n)
    def _():
        nslot = (i + 1) % 2
        _start_row_gather(posn_ref, rows, ys_hbm, buf, nslot * rows, sem.at[nslot])

    x1 = x1_ref[...]
    xb = x1.astype(BF16)
    a = jnp.dot(xb, wsg_ref[...], preferred_element_type=F32)
    u = jnp.dot(xb, wsu_ref[...], preferred_element_type=F32)
    h = (a / (1.0 + jnp.exp(-a)) * u).astype(BF16)
    ffn = jnp.dot(h, wsd_ref[...], preferred_element_type=F32)

    row0 = pl.multiple_of(slot * rows, rows)
    _wait_row_gather(rows, ys_hbm, buf, row0, sem.at[slot])
    gate = gate_ref[...]
    for k in range(TOP_K):
        ffn = ffn + buf[pl.ds(row0 + k * tt, tt), :] * gate[:, k:k + 1]
    out_ref[...] = _layer_norm(DEEPNORM_ALPHA * x1 + ffn, g_ref[...], b_ref[...])


def _combine(pos_blocks, ys, gates, x1, ws_gate, ws_up, ws_down, g, b, tt=128):
    T = x1.shape[0]
    n = T // tt
    rows = TOP_K * tt
    row = lambda i: (i, 0)
    fix = lambda i: (0, 0)
    return pl.pallas_call(
        _combine_kernel,
        grid=(n,),
        in_specs=[
            pl.BlockSpec((1, 1, rows), lambda i: (i, 0, 0), memory_space=pltpu.SMEM),
            pl.BlockSpec((1, 1, rows), lambda i: (jnp.minimum(i + 1, n - 1), 0, 0), memory_space=pltpu.SMEM),
            pl.BlockSpec(memory_space=pl.ANY),
            pl.BlockSpec((tt, LANES), row),
            pl.BlockSpec((tt, D_MODEL), row),
            pl.BlockSpec((D_MODEL, SHARED_FF), fix),
            pl.BlockSpec((D_MODEL, SHARED_FF), fix),
            pl.BlockSpec((SHARED_FF, D_MODEL), fix),
            pl.BlockSpec((1, D_MODEL), fix),
            pl.BlockSpec((1, D_MODEL), fix),
        ],
        out_specs=pl.BlockSpec((tt, D_MODEL), row),
        out_shape=jax.ShapeDtypeStruct((T, D_MODEL), F32),
        scratch_shapes=[pltpu.VMEM((2 * rows, D_MODEL), F32), pltpu.SemaphoreType.DMA((2,))],
        compiler_params=_cparams(48),
        name="moe_combine_ln2",
    )(pos_blocks, pos_blocks, ys, gates, x1, ws_gate, ws_up, ws_down, g, b)


def _rope_tables(T):
    inv = 1.0 / (ROPE_THETA ** (jnp.arange(0, HEAD_DIM, 2, dtype=F32) / HEAD_DIM))
    ang = jnp.arange(T, dtype=F32)[:, None] * inv[None, :]
    cos, sin = jnp.cos(ang), jnp.sin(ang)
    cos128 = jnp.tile(cos, (1, 4))
    sin128 = jnp.tile(jnp.concatenate([-sin, sin], axis=1), (1, 2))
    return cos128, sin128


def _swa_head_order(a, axis):
    half = SWA_Q_HEADS // 2
    shp = a.shape
    a = a.reshape(shp[:axis] + (2, half, HEAD_DIM) + shp[axis + 1:])
    return jnp.swapaxes(a, axis, axis + 1).reshape(shp)


def _block_tables(top_e, counts, T):
    A = T * TOP_K
    NB = (A + N_EXPERTS * (MOE_BLOCK - 1) + MOE_BLOCK - 1) // MOE_BLOCK
    padded = (counts + MOE_BLOCK - 1) // MOE_BLOCK * MOE_BLOCK
    starts = jnp.cumsum(counts) - counts
    pends = jnp.cumsum(padded)
    pstarts = pends - padded
    n_used = (pends[-1] // MOE_BLOCK).astype(jnp.int32).reshape(1)
    row0 = jnp.arange(NB, dtype=jnp.int32) * MOE_BLOCK
    block_e = jnp.minimum(jnp.sum((pends[None, :] <= row0[:, None]).astype(jnp.int32), axis=1), N_EXPERTS - 1)
    onehot = (block_e[:, None] == jnp.arange(N_EXPERTS, dtype=jnp.int32)[None, :]).astype(jnp.int32)
    src0 = jnp.sum(onehot * (starts - pstarts)[None, :], axis=1) + row0
    src0 = jnp.clip(src0, 0, A).astype(jnp.int32)
    tok = jnp.arange(T, dtype=jnp.int32)[:, None]
    tok_s = jnp.sort((top_e * T + tok).reshape(A)) % T
    tok_s = jnp.concatenate([tok_s, jnp.zeros((MOE_BLOCK,), jnp.int32)])
    tok_blocks = jax.vmap(lambda s: lax.dynamic_slice(tok_s, (s,), (MOE_BLOCK,)))(src0)
    return NB, block_e.astype(jnp.int32), n_used, tok_blocks, pstarts


def kernel(x, w_in, b_in, sinks, lam_q1, lam_k1, lam_q2, lam_k2, diff_norm_g, w_out, ln1_g, ln1_b,
           w_router, router_bias, w_gate, w_up, w_down, ws_gate, ws_up, ws_down, ln2_g, ln2_b):
    B, S, D = x.shape
    assert D == D_MODEL and w_in.shape[0] == DEPTH
    T = B * S
    assert B == 1 and T % 512 == 0
    cos128, sin128 = _rope_tables(S)
    x2 = x.reshape(T, D)
    for l in range(DEPTH):
        lam_init = 0.8 - 0.6 * math.exp(-0.3 * l)
        wi = w_in[l]
        wi = jnp.concatenate([_swa_head_order(wi[:, :_AK], 1), wi[:, _AK:]], axis=1).astype(BF16)
        bi = jnp.concatenate([_swa_head_order(b_in[l][:_AK], 0), b_in[l][_AK:]])[None, :]
        wo = jnp.concatenate([_swa_head_order(w_out[l][:_AK], 0), w_out[l][_AK:]], axis=0).astype(BF16)
        sink_p = sinks[l].astype(F32)
        lam = (jnp.exp(jnp.sum(lam_q1[l].astype(F32) * lam_k1[l].astype(F32)))
               - jnp.exp(jnp.sum(lam_q2[l].astype(F32) * lam_k2[l].astype(F32))) + lam_init).reshape(1)
        wr = w_router[l]
        wr_hi = wr.astype(BF16)
        wr_lo = (wr - wr_hi.astype(F32)).astype(BF16)

        qa, ka, va, qd, kd, vd = _in_proj(x2, wi, bi, cos128, sin128)
        oa = _swa(sink_p, qa, ka, va)
        ob = _diff_attn(lam, qd, kd, vd, diff_norm_g[l][None, :], lam_init)
        x1, logits = _out_proj(oa, ob, x2, wo, ln1_g[l][None, :], ln1_b[l][None, :], wr_hi, wr_lo)
        idx, gates, rank, counts = _route(logits, router_bias[l][None, :])
        NB, block_e, n_used, tok_blocks, pstarts = _block_tables(idx[:, :TOP_K], counts[0], T)
        pos = _row_pos(idx, rank, pstarts.astype(F32)[None, :])[:, :TOP_K]
        ys = _experts(block_e, n_used, tok_blocks.reshape(NB, 1, MOE_BLOCK), x1, w_gate[l], w_up[l], w_down[l])
        tt = 128
        pos_blocks = pos.reshape(T // tt, tt, TOP_K).transpose(0, 2, 1).reshape(T // tt, 1, TOP_K * tt)
        x2 = _combine(pos_blocks, ys, gates, x1, ws_gate[l].astype(BF16), ws_up[l].astype(BF16),
                      ws_down[l].astype(BF16), ln2_g[l][None, :], ln2_b[l][None, :], tt=tt)
    return x2.reshape(B, S, D)
```

```python
import functools
import math

import numpy as np
import jax
import jax.numpy as jnp
from jax import lax
from jax.experimental import pallas as pl
from jax.experimental.pallas import tpu as pltpu

D_MODEL = 1024
CHUNK = 64
HEAD_DIM = 64
ROPE_THETA = 10000.0
SWA_Q_HEADS = 8
SWA_KV_HEADS = 2
SWA_WIN_CHUNKS = 2
DIFF_HEADS = 4
DIFF_V_DIM = 2 * HEAD_DIM
N_EXPERTS = 256
TOP_K = 8
N_GROUPS = 8
TOPK_GROUPS = 4
GROUP_SIZE = N_EXPERTS // N_GROUPS
EXPERT_FF = 256
SHARED_FF = 256
ROUTED_SCALE = 2.5
MOE_BLOCK = 128
DEPTH = 1
DEEPNORM_ALPHA = (2.0 * DEPTH) ** 0.25
NORM_EPS = 1e-5
LOG2E = math.log2(math.e)
NEG_BIG = -1e30
LANES = 128
SUBLANES = 8

_AQ, _AK, _AV, _BQ, _BK, _BV, _END = 0, 512, 640, 768, 1280, 1792, 2304
IN_WIDTH = _END

_NT = (((1,), (1,)), ((), ()))

F32 = jnp.float32
BF16 = jnp.bfloat16


def _cparams(vmem_mb, n_axes=1):
    return pltpu.CompilerParams(dimension_semantics=("arbitrary",) * n_axes,
                                vmem_limit_bytes=vmem_mb * 1024 * 1024)


def _in_proj_kernel(x_ref, w_ref, b_ref, cos_ref, sin_ref,
                    qa_ref, ka_ref, va_ref, qd_ref, kd_ref, vd_ref):
    tm = x_ref.shape[0]
    x = x_ref[...].astype(BF16)
    cos = cos_ref[...]
    sin = sin_ref[...]
    lane = lax.broadcasted_iota(jnp.int32, (tm, LANES), 1)
    first_half = (lane % HEAD_DIM) < (HEAD_DIM // 2)
    q_scale = HEAD_DIM ** -0.5 * LOG2E

    def proj(c0, c1):
        return jnp.dot(x, w_ref[:, c0:c1], preferred_element_type=F32) + b_ref[:, c0:c1]

    def rope(c):
        rot = jnp.where(first_half, pltpu.roll(c, LANES - 32, 1), pltpu.roll(c, 32, 1))
        return c * cos + rot * sin

    def emit(out_ref, c0, c1, rotary, scale):
        for g0 in range(c0, c1, 256):
            g1 = min(g0 + 256, c1)
            p = proj(g0, g1)
            for j in range(0, g1 - g0, LANES):
                c = p[:, j:j + LANES]
                if rotary:
                    c = rope(c)
                if scale != 1.0:
                    c = c * scale
                out_ref[:, g0 - c0 + j:g0 - c0 + j + LANES] = c.astype(out_ref.dtype)

    emit(qa_ref, _AQ, _AK, True, q_scale)
    emit(ka_ref, _AK, _AV, True, 1.0)
    emit(va_ref, _AV, _BQ, False, 1.0)
    emit(qd_ref, _BQ, _BK, True, q_scale)
    emit(kd_ref, _BK, _BV, True, 1.0)
    emit(vd_ref, _BV, _END, False, 1.0)


def _in_proj(x2, w_in, b_in, cos128, sin128, tm=512):
    T = x2.shape[0]
    outs = [(T, 512), (T, 128), (T, 128), (T, 512), (T, 512), (T, 512)]
    return pl.pallas_call(
        _in_proj_kernel,
        grid=(T // tm,),
        in_specs=[
            pl.BlockSpec((tm, D_MODEL), lambda i: (i, 0)),
            pl.BlockSpec((D_MODEL, IN_WIDTH), lambda i: (0, 0)),
            pl.BlockSpec((1, IN_WIDTH), lambda i: (0, 0)),
            pl.BlockSpec((tm, LANES), lambda i: (i, 0)),
            pl.BlockSpec((tm, LANES), lambda i: (i, 0)),
        ],
        out_specs=[pl.BlockSpec((tm, s[1]), lambda i: (i, 0)) for s in outs],
        out_shape=[jax.ShapeDtypeStruct(s, BF16) for s in outs],
        compiler_params=_cparams(48),
        name="in_proj_rope",
    )(x2, w_in, b_in, cos128, sin128)


def _swa_kernel(sink_ref, q_ref, kp_ref, kc_ref, vp_ref, vc_ref, o_ref):
    i = pl.program_id(0)
    tq = q_ref.shape[0]
    back = SWA_WIN_CHUNKS * CHUNK
    keys = jnp.concatenate([kp_ref[tq - back:, :], kc_ref[...]], axis=0)
    vals = jnp.concatenate([vp_ref[tq - back:, :], vc_ref[...]], axis=0)
    nk = tq + back
    qc = lax.broadcasted_iota(jnp.int32, (tq, nk), 0) // CHUNK
    kc = lax.broadcasted_iota(jnp.int32, (tq, nk), 1) // CHUNK - SWA_WIN_CHUNKS
    lo = jnp.where(i > 0, qc - SWA_WIN_CHUNKS, jnp.maximum(qc - SWA_WIN_CHUNKS, 0))
    bias = jnp.where(kc <= qc, jnp.where(kc >= lo, 0.0, NEG_BIG), NEG_BIG).astype(F32)
    lane = lax.broadcasted_iota(jnp.int32, (tq, LANES), 1)
    low = lane < HEAD_DIM
    zero = jnp.zeros((), BF16)
    for c in range(SWA_Q_HEADS // 2):
        qch = q_ref[:, c * LANES:(c + 1) * LANES]
        outs = []
        for half in range(2):
            h = c + (SWA_Q_HEADS // 2) * half
            qm = jnp.where(low if half == 0 else jnp.logical_not(low), qch, zero)
            s = lax.dot_general(qm, keys, _NT, preferred_element_type=F32) + bias
            sink = sink_ref[h] * LOG2E
            m = jnp.maximum(jnp.max(s, axis=-1, keepdims=True), sink)
            p = jnp.exp2(s - m)
            denom = jnp.sum(p, axis=-1, keepdims=True) + jnp.exp2(sink - m)
            o = jnp.dot(p.astype(BF16), vals, preferred_element_type=F32)
            outs.append(o / denom)
        o_ref[:, c * LANES:(c + 1) * LANES] = jnp.where(low, outs[0], outs[1]).astype(o_ref.dtype)


def _swa(sinks, qa, ka, va, tq=256):
    T = qa.shape[0]
    prev = lambda i: (jnp.maximum(i - 1, 0), 0)
    cur = lambda i: (i, 0)
    return pl.pallas_call(
        _swa_kernel,
        grid=(T // tq,),
        in_specs=[
            pl.BlockSpec(memory_space=pltpu.SMEM),
            pl.BlockSpec((tq, 512), cur),
            pl.BlockSpec((tq, LANES), prev),
            pl.BlockSpec((tq, LANES), cur),
            pl.BlockSpec((tq, LANES), prev),
            pl.BlockSpec((tq, LANES), cur),
        ],
        out_specs=pl.BlockSpec((tq, 512), cur),
        out_shape=jax.ShapeDtypeStruct((T, 512), BF16),
        compiler_params=_cparams(32),
        name="swa_sink_attention",
    )(sinks, qa, ka, ka, va, va)


def _diff_kernel(lam_ref, q_ref, k_ref, v_ref, g_ref, o_ref, acc_ref, *, lam_init):
    i = pl.program_id(1)
    tq = q_ref.shape[0]
    tk = tq
    q = q_ref[...]
    lane = lax.broadcasted_iota(jnp.int32, (tq, LANES), 1)
    zero = jnp.zeros((), BF16)
    qs = (jnp.where(lane < HEAD_DIM, q, zero), jnp.where(lane >= HEAD_DIM, q, zero))
    ones_col = jnp.where(lax.broadcasted_iota(jnp.int32, (tk, LANES), 1) == 0, 1.0, 0.0).astype(BF16)
    acc_ref[...] = jnp.zeros_like(acc_ref)

    def step(j, carry, diagonal):
        start = pl.multiple_of(j * tk, tk)
        k = k_ref[pl.ds(start, tk), :]
        v = v_ref[pl.ds(start, tk), :]
        vext = jnp.concatenate([v, ones_col], axis=1)
        new = []
        for mi in range(2):
            s = lax.dot_general(qs[mi], k, _NT, preferred_element_type=F32)
            if diagonal:
                rc = lax.broadcasted_iota(jnp.int32, (tq, tk), 0) // CHUNK
                cc = lax.broadcasted_iota(jnp.int32, (tq, tk), 1) // CHUNK
                s = jnp.where(cc <= rc, s, NEG_BIG)
            m_old = carry[mi]
            m_new = jnp.maximum(m_old, jnp.max(s, axis=-1, keepdims=True))
            alpha = jnp.exp2(m_old - m_new)
            p = jnp.exp2(s - m_new).astype(BF16)
            acc_ref[mi] = acc_ref[mi] * alpha + jnp.dot(p, vext, preferred_element_type=F32)
            new.append(m_new)
        return tuple(new)

    init = (jnp.full((tq, 1), NEG_BIG, F32), jnp.full((tq, 1), NEG_BIG, F32))
    carry = lax.fori_loop(0, i, lambda j, c: step(j, c, False), init)
    step(i, carry, True)

    a0 = acc_ref[0]
    a1 = acc_ref[1]
    lam = lam_ref[0]
    o = a0[:, :DIFF_V_DIM] / a0[:, DIFF_V_DIM:DIFF_V_DIM + 1] - lam * (a1[:, :DIFF_V_DIM] / a1[:, DIFF_V_DIM:DIFF_V_DIM + 1])
    ms = jnp.mean(o * o, axis=-1, keepdims=True)
    o = o * lax.rsqrt(ms + NORM_EPS) * g_ref[...] * (1.0 - lam_init)
    o_ref[...] = o.astype(o_ref.dtype)


def _diff_attn(lam, qd, kd, vd, sub_g, lam_init, tq=512):
    T = qd.shape[0]
    return pl.pallas_call(
        functools.partial(_diff_kernel, lam_init=lam_init),
        grid=(DIFF_HEADS, T // tq),
        in_specs=[
            pl.BlockSpec(memory_space=pltpu.SMEM),
            pl.BlockSpec((tq, LANES), lambda h, i: (i, h)),
            pl.BlockSpec((T, LANES), lambda h, i: (0, h)),
            pl.BlockSpec((T, LANES), lambda h, i: (0, h)),
            pl.BlockSpec((1, LANES), lambda h, i: (0, 0)),
        ],
        out_specs=pl.BlockSpec((tq, LANES), lambda h, i: (i, h)),
        out_shape=jax.ShapeDtypeStruct((T, DIFF_HEADS * DIFF_V_DIM), BF16),
        scratch_shapes=[pltpu.VMEM((2, tq, 2 * LANES), F32)],
        compiler_params=_cparams(48, 2),
        name="diff_attention",
    )(lam, qd, kd, vd, sub_g)


def _from_token_tiles(v):
    v = jnp.swapaxes(v, 0, 1)
    return jnp.concatenate([v[s] for s in range(SUBLANES)], axis=1)


def _to_token_tiles(v):
    return jnp.swapaxes(jnp.stack([v[:, s * LANES:(s + 1) * LANES] for s in range(SUBLANES)], axis=0), 0, 1)


def _layer_norm(y, g, b):
    mu = jnp.mean(y, axis=-1, keepdims=True)
    d = y - mu
    var = jnp.mean(d * d, axis=-1, keepdims=True)
    return d * lax.rsqrt(var + NORM_EPS) * g + b


def _out_proj_kernel(oa_ref, ob_ref, x_ref, wo_ref, g_ref, b_ref, wrh_ref, wrl_ref, x1_ref, x1t_ref, logit_ref):
    half = oa_ref.shape[1]
    mix = jnp.dot(oa_ref[...], wo_ref[:half, :], preferred_element_type=F32)
    mix = mix + jnp.dot(ob_ref[...], wo_ref[half:, :], preferred_element_type=F32)
    x1 = _layer_norm(DEEPNORM_ALPHA * x_ref[...] + mix, g_ref[...], b_ref[...])
    x1_ref[...] = x1
    x1t_ref[...] = _to_token_tiles(x1)
    xh = x1.astype(BF16)
    xl = (x1 - xh.astype(F32)).astype(BF16)
    wh = wrh_ref[...]
    logits = jnp.dot(xh, wh, preferred_element_type=F32)
    logits = logits + jnp.dot(xl, wh, preferred_element_type=F32)
    logits = logits + jnp.dot(xh, wrl_ref[...], preferred_element_type=F32)
    logit_ref[...] = logits


def _out_proj(oa, ob, x2, w_out, g, b, wr_hi, wr_lo, tm=512):
    T = x2.shape[0]
    row = lambda i: (i, 0)
    fix = lambda i: (0, 0)
    return pl.pallas_call(
        _out_proj_kernel,
        grid=(T // tm,),
        in_specs=[
            pl.BlockSpec((tm, 512), row),
            pl.BlockSpec((tm, 512), row),
            pl.BlockSpec((tm, D_MODEL), row),
            pl.BlockSpec((D_MODEL, D_MODEL), fix),
            pl.BlockSpec((1, D_MODEL), fix),
            pl.BlockSpec((1, D_MODEL), fix),
            pl.BlockSpec((D_MODEL, N_EXPERTS), fix),
            pl.BlockSpec((D_MODEL, N_EXPERTS), fix),
        ],
        out_specs=[pl.BlockSpec((tm, D_MODEL), row), pl.BlockSpec((tm, SUBLANES, LANES), lambda i: (i, 0, 0)),
                   pl.BlockSpec((tm, N_EXPERTS), row)],
        out_shape=[jax.ShapeDtypeStruct((T, D_MODEL), F32), jax.ShapeDtypeStruct((T, SUBLANES, LANES), F32),
                   jax.ShapeDtypeStruct((T, N_EXPERTS), F32)],
        compiler_params=_cparams(48),
        name="out_proj_ln1_router",
    )(oa, ob, x2, w_out, g, b, wr_hi, wr_lo)


def _route_kernel(logit_ref, bias_ref, idx_ref, gate_ref, rank_ref, count_ref, run_ref):
    i = pl.program_id(0)
    tm = logit_ref.shape[0]
    scores = 1.0 / (1.0 + jnp.exp(-logit_ref[...]))
    sel = scores + bias_ref[...]
    lane = lax.broadcasted_iota(jnp.int32, (tm, N_EXPERTS), 1)
    lanef = lane.astype(F32)
    grp = lane // GROUP_SIZE
    neg_inf = -jnp.inf

    @pl.when(i == 0)
    def _():
        run_ref[...] = jnp.zeros_like(run_ref)

    def first_argmax(v):
        m = jnp.max(v, axis=-1, keepdims=True)
        idx = jnp.min(jnp.where(v == m, lanef, float(N_EXPERTS)), axis=-1, keepdims=True)
        return m, idx

    gscore = []
    for g in range(N_GROUPS):
        vg = jnp.where(grp == g, sel, neg_inf)
        m1, i1 = first_argmax(vg)
        m2 = jnp.max(jnp.where(lanef == i1, neg_inf, vg), axis=-1, keepdims=True)
        gscore.append(m1 + m2)
    keepf = jnp.zeros((tm, N_EXPERTS), F32)
    for g in range(N_GROUPS):
        ahead_n = jnp.zeros((tm, 1), F32)
        for o in range(N_GROUPS):
            if o == g:
                continue
            ahead = (gscore[o] > gscore[g]) if o > g else (gscore[o] >= gscore[g])
            ahead_n = ahead_n + jnp.where(ahead, 1.0, 0.0)
        keepf = jnp.where(grp == g, jnp.where(ahead_n < float(TOPK_GROUPS), 1.0, 0.0), keepf)
    cand = jnp.where(keepf > 0.5, sel, neg_inf)

    out_lane = lax.broadcasted_iota(jnp.int32, (tm, LANES), 1)
    idx_acc = jnp.zeros((tm, LANES), F32)
    sc_acc = jnp.zeros((tm, LANES), F32)
    hits = []
    chosen = jnp.zeros((tm, N_EXPERTS), F32)
    for k in range(TOP_K):
        _, ik = first_argmax(cand)
        hit = lanef == ik
        hits.append(hit)
        chosen = jnp.where(hit, 1.0, chosen)
        sk = jnp.sum(jnp.where(hit, scores, 0.0), axis=-1, keepdims=True)
        idx_acc = jnp.where(out_lane == k, ik, idx_acc)
        sc_acc = jnp.where(out_lane == k, sk, sc_acc)
        cand = jnp.where(hit, neg_inf, cand)
    total = jnp.sum(sc_acc, axis=-1, keepdims=True)
    idx_ref[...] = idx_acc.astype(jnp.int32)
    gate_ref[...] = sc_acc / total * ROUTED_SCALE

    r = lax.broadcasted_iota(jnp.int32, (tm, tm), 0)
    c = lax.broadcasted_iota(jnp.int32, (tm, tm), 1)
    lower = jnp.where(c < r, 1.0, 0.0).astype(BF16)
    before = jnp.dot(lower, chosen.astype(BF16), preferred_element_type=F32) + run_ref[...]
    rank_acc = jnp.zeros((tm, LANES), F32)
    for k in range(TOP_K):
        rk = jnp.sum(jnp.where(hits[k], before, 0.0), axis=-1, keepdims=True)
        rank_acc = jnp.where(out_lane == k, rk, rank_acc)
    rank_ref[...] = rank_acc.astype(jnp.int32)
    run_ref[...] = run_ref[...] + jnp.sum(chosen, axis=0, keepdims=True)
    count_ref[...] = run_ref[...].astype(jnp.int32)


def _route(logits, r_bias, tm=512):
    T = logits.shape[0]
    row = lambda i: (i, 0)
    fix = lambda i: (0, 0)
    return pl.pallas_call(
        _route_kernel,
        grid=(T // tm,),
        in_specs=[pl.BlockSpec((tm, N_EXPERTS), row), pl.BlockSpec((1, N_EXPERTS), fix)],
        out_specs=[pl.BlockSpec((tm, LANES), row), pl.BlockSpec((tm, LANES), row), pl.BlockSpec((tm, LANES), row),
                   pl.BlockSpec((1, N_EXPERTS), fix)],
        out_shape=[jax.ShapeDtypeStruct((T, LANES), jnp.int32), jax.ShapeDtypeStruct((T, LANES), F32),
                   jax.ShapeDtypeStruct((T, LANES), jnp.int32), jax.ShapeDtypeStruct((1, N_EXPERTS), jnp.int32)],
        scratch_shapes=[pltpu.VMEM((1, N_EXPERTS), F32)],
        compiler_params=_cparams(32),
        name="moe_route",
    )(logits, r_bias)


def _row_pos_kernel(idx_ref, rank_ref, pstart_ref, pos_ref):
    tm = idx_ref.shape[0]
    lanef = lax.broadcasted_iota(jnp.int32, (tm, N_EXPERTS), 1).astype(F32)
    out_lane = lax.broadcasted_iota(jnp.int32, (tm, LANES), 1)
    idx = idx_ref[...].astype(F32)
    pstart = pstart_ref[...]
    base = jnp.zeros((tm, LANES), F32)
    for k in range(TOP_K):
        ik = idx[:, k:k + 1]
        pk = jnp.sum(jnp.where(lanef == ik, pstart, 0.0), axis=-1, keepdims=True)
        base = jnp.where(out_lane == k, pk, base)
    pos_ref[...] = base.astype(jnp.int32) + rank_ref[...]


def _row_pos(idx, rank, pstarts, tm=512):
    T = idx.shape[0]
    row = lambda i: (i, 0)
    return pl.pallas_call(
        _row_pos_kernel,
        grid=(T // tm,),
        in_specs=[pl.BlockSpec((tm, LANES), row), pl.BlockSpec((tm, LANES), row),
                  pl.BlockSpec((1, N_EXPERTS), lambda i: (0, 0))],
        out_specs=pl.BlockSpec((tm, LANES), row),
        out_shape=jax.ShapeDtypeStruct((T, LANES), jnp.int32),
        compiler_params=_cparams(32),
        name="moe_row_pos",
    )(idx, rank, pstarts)


def _start_row_gather(idx_ref, n_rows, src_hbm, dst, row0, sem, unrolled=False):
    def start(r):
        t = idx_ref[0, 0, r]
        pltpu.make_async_copy(src_hbm.at[t], dst.at[row0 + r], sem).start()

    if unrolled:
        for r in range(n_rows):
            start(r)
    else:
        def body(r, carry):
            start(r)
            return carry
        lax.fori_loop(0, n_rows, body, 0, unroll=8)


def _wait_row_gather(n_rows, src_hbm, dst, row0, sem):
    pltpu.make_async_copy(src_hbm.at[pl.ds(0, n_rows)], dst.at[pl.ds(row0, n_rows)], sem).wait()


GATHER_SLOTS = 3


def _expert_kernel(blk_ref, e_ref, lo_ref, hi_ref, tok0_ref, tok1_ref, tok2_ref, gate_ref, x_hbm,
                   wg_ref, wu_ref, wd_ref, y_ref, buf, sem):
    del e_ref
    i = pl.program_id(0)
    n = pl.num_programs(0)
    rows = MOE_BLOCK

    @pl.when(i == 0)
    def _():
        _start_row_gather(tok0_ref, rows, x_hbm, buf, 0, sem.at[0])
        _start_row_gather(tok1_ref, rows, x_hbm, buf, rows, sem.at[1])

    @pl.when(jnp.logical_or(i == 0, blk_ref[i] != blk_ref[jnp.maximum(i - 1, 0)]))
    def _():
        y_ref[...] = jnp.zeros_like(y_ref)

    slot = i % GATHER_SLOTS
    row0 = pl.multiple_of(slot * rows, rows)
    _wait_row_gather(rows, x_hbm, buf, row0, sem.at[slot])
    xg = _from_token_tiles(buf[pl.ds(row0, rows)]).astype(BF16)
    nslot = (i + 2) % GATHER_SLOTS
    _start_row_gather(tok2_ref, rows, x_hbm, buf, nslot * rows, sem.at[nslot], unrolled=True)
    a = jnp.dot(xg, wg_ref[...].astype(BF16), preferred_element_type=F32)
    u = jnp.dot(xg, wu_ref[...].astype(BF16), preferred_element_type=F32)
    h = (a / (1.0 + jnp.exp(-a)) * u).astype(BF16)
    y = jnp.dot(h, wd_ref[...].astype(BF16), preferred_element_type=F32)
    r = lax.broadcasted_iota(jnp.int32, (rows, 1), 0)
    mine = jnp.logical_and(r >= lo_ref[i], r < hi_ref[i])
    y_ref[...] += _to_token_tiles(jnp.where(mine, y * gate_ref[0], 0.0))

    @pl.when(i == n - 1)
    def _():
        for d in (1, 2):
            s = (i + d) % GATHER_SLOTS
            _wait_row_gather(rows, x_hbm, buf, s * rows, sem.at[s])


def _experts(items, tok_blocks, gate_blocks, x1, w_gate, w_up, w_down):
    it_blk, it_e, it_lo, it_hi = items
    NI = it_blk.shape[0]
    NBLK = tok_blocks.shape[0]
    ahead = lambda d: (lambda i, blk, e, lo, hi: (blk[jnp.minimum(i + d, NI - 1)], 0, 0))
    wsel = lambda i, blk, e, lo, hi: (e[i], 0, 0)
    grid_spec = pltpu.PrefetchScalarGridSpec(
        num_scalar_prefetch=4,
        grid=(NI,),
        in_specs=[
            pl.BlockSpec((1, 1, MOE_BLOCK), ahead(0), memory_space=pltpu.SMEM),
            pl.BlockSpec((1, 1, MOE_BLOCK), ahead(1), memory_space=pltpu.SMEM),
            pl.BlockSpec((1, 1, MOE_BLOCK), ahead(2), memory_space=pltpu.SMEM),
            pl.BlockSpec((1, MOE_BLOCK, 1), ahead(0)),
            pl.BlockSpec(memory_space=pl.ANY),
            pl.BlockSpec((None, D_MODEL, EXPERT_FF), wsel),
            pl.BlockSpec((None, D_MODEL, EXPERT_FF), wsel),
            pl.BlockSpec((None, EXPERT_FF, D_MODEL), wsel),
        ],
        out_specs=pl.BlockSpec((MOE_BLOCK, SUBLANES, LANES), lambda i, blk, e, lo, hi: (blk[i], 0, 0)),
        scratch_shapes=[pltpu.VMEM((GATHER_SLOTS * MOE_BLOCK, SUBLANES, LANES), F32),
                        pltpu.SemaphoreType.DMA((GATHER_SLOTS,))],
    )
    return pl.pallas_call(
        _expert_kernel,
        grid_spec=grid_spec,
        out_shape=jax.ShapeDtypeStruct((NBLK * MOE_BLOCK, SUBLANES, LANES), F32),
        compiler_params=_cparams(48),
        name="moe_experts",
    )(it_blk, it_e, it_lo, it_hi, tok_blocks, tok_blocks, tok_blocks, gate_blocks, x1, w_gate, w_up, w_down)


def _combine_kernel(posc_ref, posn_ref, ys_hbm, x1_ref, wsg_ref, wsu_ref, wsd_ref, g_ref, b_ref, out_ref, buf, sem):
    i = pl.program_id(0)
    n = pl.num_programs(0)
    tt = x1_ref.shape[0]
    rows = TOP_K * tt
    slot = i % 2

    @pl.when(i == 0)
    def _():
        _start_row_gather(posc_ref, rows, ys_hbm, buf, 0, sem.at[0])

    @pl.when(i + 1 < n)
    def _():
        nslot = (i + 1) % 2
        _start_row_gather(posn_ref, rows, ys_hbm, buf, nslot * rows, sem.at[nslot])

    x1 = x1_ref[...]
    xb = x1.astype(BF16)
    a = jnp.dot(xb, wsg_ref[...], preferred_element_type=F32)
    u = jnp.dot(xb, wsu_ref[...], preferred_element_type=F32)
    h = (a / (1.0 + jnp.exp(-a)) * u).astype(BF16)
    ffn = jnp.dot(h, wsd_ref[...], preferred_element_type=F32)

    row0 = pl.multiple_of(slot * rows, rows)
    _wait_row_gather(rows, ys_hbm, buf, row0, sem.at[slot])
    routed = buf[pl.ds(row0, tt)]
    for k in range(1, TOP_K):
        routed = routed + buf[pl.ds(row0 + k * tt, tt)]
    ffn = ffn + _from_token_tiles(routed)
    out_ref[...] = _layer_norm(DEEPNORM_ALPHA * x1 + ffn, g_ref[...], b_ref[...])


def _combine(pos_blocks, ys, x1, ws_gate, ws_up, ws_down, g, b, tt=128):
    T = x1.shape[0]
    n = T // tt
    rows = TOP_K * tt
    row = lambda i: (i, 0)
    fix = lambda i: (0, 0)
    return pl.pallas_call(
        _combine_kernel,
        grid=(n,),
        in_specs=[
            pl.BlockSpec((1, 1, rows), lambda i: (i, 0, 0), memory_space=pltpu.SMEM),
            pl.BlockSpec((1, 1, rows), lambda i: (jnp.minimum(i + 1, n - 1), 0, 0), memory_space=pltpu.SMEM),
            pl.BlockSpec(memory_space=pl.ANY),
            pl.BlockSpec((tt, D_MODEL), row),
            pl.BlockSpec((D_MODEL, SHARED_FF), fix),
            pl.BlockSpec((D_MODEL, SHARED_FF), fix),
            pl.BlockSpec((SHARED_FF, D_MODEL), fix),
            pl.BlockSpec((1, D_MODEL), fix),
            pl.BlockSpec((1, D_MODEL), fix),
        ],
        out_specs=pl.BlockSpec((tt, D_MODEL), row),
        out_shape=jax.ShapeDtypeStruct((T, D_MODEL), F32),
        scratch_shapes=[pltpu.VMEM((2 * rows, SUBLANES, LANES), F32), pltpu.SemaphoreType.DMA((2,))],
        compiler_params=_cparams(48),
        name="moe_combine_ln2",
    )(pos_blocks, pos_blocks, ys, x1, ws_gate, ws_up, ws_down, g, b)


def _rope_tables(T):
    inv = 1.0 / (ROPE_THETA ** (jnp.arange(0, HEAD_DIM, 2, dtype=F32) / HEAD_DIM))
    ang = jnp.arange(T, dtype=F32)[:, None] * inv[None, :]
    cos, sin = jnp.cos(ang), jnp.sin(ang)
    cos128 = jnp.tile(cos, (1, 4))
    sin128 = jnp.tile(jnp.concatenate([-sin, sin], axis=1), (1, 2))
    return cos128, sin128


def _swa_head_order(a, axis):
    half = SWA_Q_HEADS // 2
    shp = a.shape
    a = a.reshape(shp[:axis] + (2, half, HEAD_DIM) + shp[axis + 1:])
    return jnp.swapaxes(a, axis, axis + 1).reshape(shp)


def _item_tables(top_e, gate, counts, T):
    A = T * TOP_K
    NBLK = A // MOE_BLOCK
    NI = NBLK + N_EXPERTS - 1
    starts = jnp.cumsum(counts) - counts
    ends = starts + counts
    first_blk = starts // MOE_BLOCK
    nblk = jnp.where(counts > 0, (ends - 1) // MOE_BLOCK - first_blk + 1, 0)
    item_end = jnp.cumsum(nblk)
    item_off = item_end - nblk
    it = jnp.arange(NI, dtype=jnp.int32)
    live = it < item_end[-1]
    it_e = jnp.minimum(jnp.sum((item_end[None, :] <= it[:, None]).astype(jnp.int32), axis=1), N_EXPERTS - 1)
    onehot = (it_e[:, None] == jnp.arange(N_EXPERTS, dtype=jnp.int32)[None, :]).astype(jnp.int32)
    pick = lambda v: jnp.sum(onehot * v[None, :].astype(jnp.int32), axis=1)
    it_blk = pick(first_blk) + it - pick(item_off)
    it_lo = jnp.clip(pick(starts) - it_blk * MOE_BLOCK, 0, MOE_BLOCK)
    it_hi = jnp.clip(pick(ends) - it_blk * MOE_BLOCK, 0, MOE_BLOCK)
    it_blk = jnp.where(live, it_blk, NBLK - 1)
    it_lo = jnp.where(live, it_lo, 0)
    it_hi = jnp.where(live, it_hi, 0)
    tok = jnp.arange(T, dtype=jnp.int32)[:, None]
    key_s, gate_s = lax.sort(((top_e * T + tok).reshape(A), gate.reshape(A)), num_keys=1)
    items = tuple(v.astype(jnp.int32) for v in (it_blk, it_e, it_lo, it_hi))
    return items, (key_s % T).reshape(NBLK, 1, MOE_BLOCK), gate_s.reshape(NBLK, MOE_BLOCK, 1), starts


def kernel(x, w_in, b_in, sinks, lam_q1, lam_k1, lam_q2, lam_k2, diff_norm_g, w_out, ln1_g, ln1_b,
           w_router, router_bias, w_gate, w_up, w_down, ws_gate, ws_up, ws_down, ln2_g, ln2_b):
    B, S, D = x.shape
    assert D == D_MODEL and w_in.shape[0] == DEPTH
    T = B * S
    assert B == 1 and T % 512 == 0
    cos128, sin128 = _rope_tables(S)
    x2 = x.reshape(T, D)
    for l in range(DEPTH):
        lam_init = 0.8 - 0.6 * math.exp(-0.3 * l)
        wi = w_in[l]
        wi = jnp.concatenate([_swa_head_order(wi[:, :_AK], 1), wi[:, _AK:]], axis=1).astype(BF16)
        bi = jnp.concatenate([_swa_head_order(b_in[l][:_AK], 0), b_in[l][_AK:]])[None, :]
        wo = jnp.concatenate([_swa_head_order(w_out[l][:_AK], 0), w_out[l][_AK:]], axis=0).astype(BF16)
        sink_p = sinks[l].astype(F32)
        lam = (jnp.exp(jnp.sum(lam_q1[l].astype(F32) * lam_k1[l].astype(F32)))
               - jnp.exp(jnp.sum(lam_q2[l].astype(F32) * lam_k2[l].astype(F32))) + lam_init).reshape(1)
        wr = w_router[l]
        wr_hi = wr.astype(BF16)
        wr_lo = (wr - wr_hi.astype(F32)).astype(BF16)

        qa, ka, va, qd, kd, vd = _in_proj(x2, wi, bi, cos128, sin128)
        oa = _swa(sink_p, qa, ka, va)
        ob = _diff_attn(lam, qd, kd, vd, diff_norm_g[l][None, :], lam_init)
        x1, x1t, logits = _out_proj(oa, ob, x2, wo, ln1_g[l][None, :], ln1_b[l][None, :], wr_hi, wr_lo)
        idx, gates, rank, counts = _route(logits, router_bias[l][None, :])
        items, tok_blocks, gate_blocks, starts = _item_tables(idx[:, :TOP_K], gates[:, :TOP_K], counts[0], T)
        pos = _row_pos(idx, rank, starts.astype(F32)[None, :])[:, :TOP_K]
        ys = _experts(items, tok_blocks, gate_blocks, x1t, w_gate[l], w_up[l], w_down[l])
        tt = 128
        pos_blocks = pos.reshape(T // tt, tt, TOP_K).transpose(0, 2, 1).reshape(T // tt, 1, TOP_K * tt)
        x2 = _combine(pos_blocks, ys, x1, ws_gate[l].astype(BF16), ws_up[l].astype(BF16),
                      ws_down[l].astype(BF16), ln2_g[l][None, :], ln2_b[l][None, :], tt=tt)
    return x2.reshape(B, S, D)
```

```python
import functools
import math

import numpy as np
import jax
import jax.numpy as jnp
from jax import lax
from jax.experimental import pallas as pl
from jax.experimental.pallas import tpu as pltpu

D_MODEL = 1024
CHUNK = 64
HEAD_DIM = 64
ROPE_THETA = 10000.0
SWA_Q_HEADS = 8
SWA_KV_HEADS = 2
SWA_WIN_CHUNKS = 2
DIFF_HEADS = 4
DIFF_V_DIM = 2 * HEAD_DIM
N_EXPERTS = 256
TOP_K = 8
N_GROUPS = 8
TOPK_GROUPS = 4
GROUP_SIZE = N_EXPERTS // N_GROUPS
EXPERT_FF = 256
SHARED_FF = 256
ROUTED_SCALE = 2.5
MOE_BLOCK = 128
DEPTH = 1
DEEPNORM_ALPHA = (2.0 * DEPTH) ** 0.25
NORM_EPS = 1e-5
LOG2E = math.log2(math.e)
NEG_BIG = -1e30
LANES = 128
SUBLANES = 8

_AQ, _AK, _AV, _BQ, _BK, _BV, _END = 0, 512, 640, 768, 1280, 1792, 2304
IN_WIDTH = _END

_NT = (((1,), (1,)), ((), ()))

F32 = jnp.float32
BF16 = jnp.bfloat16


def _cparams(vmem_mb, n_axes=1):
    return pltpu.CompilerParams(dimension_semantics=("arbitrary",) * n_axes,
                                vmem_limit_bytes=vmem_mb * 1024 * 1024)


def _in_proj_kernel(x_ref, w_ref, b_ref, cos_ref, sin_ref,
                    qa_ref, ka_ref, va_ref, qd_ref, kd_ref, vd_ref):
    tm = x_ref.shape[0]
    x = x_ref[...].astype(BF16)
    cos = cos_ref[...]
    sin = sin_ref[...]
    lane = lax.broadcasted_iota(jnp.int32, (tm, LANES), 1)
    first_half = (lane % HEAD_DIM) < (HEAD_DIM // 2)
    q_scale = HEAD_DIM ** -0.5 * LOG2E

    def proj(c0, c1):
        return jnp.dot(x, w_ref[:, c0:c1], preferred_element_type=F32) + b_ref[:, c0:c1]

    def rope(c):
        rot = jnp.where(first_half, pltpu.roll(c, LANES - 32, 1), pltpu.roll(c, 32, 1))
        return c * cos + rot * sin

    def emit(out_ref, c0, c1, rotary, scale):
        for g0 in range(c0, c1, 256):
            g1 = min(g0 + 256, c1)
            p = proj(g0, g1)
            for j in range(0, g1 - g0, LANES):
                c = p[:, j:j + LANES]
                if rotary:
                    c = rope(c)
                if scale != 1.0:
                    c = c * scale
                out_ref[:, g0 - c0 + j:g0 - c0 + j + LANES] = c.astype(out_ref.dtype)

    emit(qa_ref, _AQ, _AK, True, q_scale)
    emit(ka_ref, _AK, _AV, True, 1.0)
    emit(va_ref, _AV, _BQ, False, 1.0)
    emit(qd_ref, _BQ, _BK, True, q_scale)
    emit(kd_ref, _BK, _BV, True, 1.0)
    emit(vd_ref, _BV, _END, False, 1.0)


def _in_proj(x2, w_in, b_in, cos128, sin128, tm=512):
    T = x2.shape[0]
    outs = [(T, 512), (T, 128), (T, 128), (T, 512), (T, 512), (T, 512)]
    return pl.pallas_call(
        _in_proj_kernel,
        grid=(T // tm,),
        in_specs=[
            pl.BlockSpec((tm, D_MODEL), lambda i: (i, 0)),
            pl.BlockSpec((D_MODEL, IN_WIDTH), lambda i: (0, 0)),
            pl.BlockSpec((1, IN_WIDTH), lambda i: (0, 0)),
            pl.BlockSpec((tm, LANES), lambda i: (i, 0)),
            pl.BlockSpec((tm, LANES), lambda i: (i, 0)),
        ],
        out_specs=[pl.BlockSpec((tm, s[1]), lambda i: (i, 0)) for s in outs],
        out_shape=[jax.ShapeDtypeStruct(s, BF16) for s in outs],
        compiler_params=_cparams(48),
        name="in_proj_rope",
    )(x2, w_in, b_in, cos128, sin128)


def _swa_kernel(sink_ref, q_ref, kp_ref, kc_ref, vp_ref, vc_ref, o_ref):
    i = pl.program_id(0)
    tq = q_ref.shape[0]
    back = SWA_WIN_CHUNKS * CHUNK
    keys = jnp.concatenate([kp_ref[tq - back:, :], kc_ref[...]], axis=0)
    vals = jnp.concatenate([vp_ref[tq - back:, :], vc_ref[...]], axis=0)
    nk = tq + back
    qc = lax.broadcasted_iota(jnp.int32, (tq, nk), 0) // CHUNK
    kc = lax.broadcasted_iota(jnp.int32, (tq, nk), 1) // CHUNK - SWA_WIN_CHUNKS
    lo = jnp.where(i > 0, qc - SWA_WIN_CHUNKS, jnp.maximum(qc - SWA_WIN_CHUNKS, 0))
    bias = jnp.where(kc <= qc, jnp.where(kc >= lo, 0.0, NEG_BIG), NEG_BIG).astype(F32)
    lane = lax.broadcasted_iota(jnp.int32, (tq, LANES), 1)
    low = lane < HEAD_DIM
    zero = jnp.zeros((), BF16)
    for c in range(SWA_Q_HEADS // 2):
        qch = q_ref[:, c * LANES:(c + 1) * LANES]
        outs = []
        for half in range(2):
            h = c + (SWA_Q_HEADS // 2) * half
            qm = jnp.where(low if half == 0 else jnp.logical_not(low), qch, zero)
            s = lax.dot_general(qm, keys, _NT, preferred_element_type=F32) + bias
            sink = sink_ref[h] * LOG2E
            m = jnp.maximum(jnp.max(s, axis=-1, keepdims=True), sink)
            p = jnp.exp2(s - m)
            denom = jnp.sum(p, axis=-1, keepdims=True) + jnp.exp2(sink - m)
            o = jnp.dot(p.astype(BF16), vals, preferred_element_type=F32)
            outs.append(o / denom)
        o_ref[:, c * LANES:(c + 1) * LANES] = jnp.where(low, outs[0], outs[1]).astype(o_ref.dtype)


def _swa(sinks, qa, ka, va, tq=256):
    T = qa.shape[0]
    prev = lambda i: (jnp.maximum(i - 1, 0), 0)
    cur = lambda i: (i, 0)
    return pl.pallas_call(
        _swa_kernel,
        grid=(T // tq,),
        in_specs=[
            pl.BlockSpec(memory_space=pltpu.SMEM),
            pl.BlockSpec((tq, 512), cur),
            pl.BlockSpec((tq, LANES), prev),
            pl.BlockSpec((tq, LANES), cur),
            pl.BlockSpec((tq, LANES), prev),
            pl.BlockSpec((tq, LANES), cur),
        ],
        out_specs=pl.BlockSpec((tq, 512), cur),
        out_shape=jax.ShapeDtypeStruct((T, 512), BF16),
        compiler_params=_cparams(32),
        name="swa_sink_attention",
    )(sinks, qa, ka, ka, va, va)


def _diff_kernel(lam_ref, q_ref, k_ref, v_ref, g_ref, o_ref, acc_ref, *, lam_init, nh, tk):
    i = pl.program_id(1)
    tq = q_ref.shape[0]
    lane = lax.broadcasted_iota(jnp.int32, (tq, LANES), 1)
    zero = jnp.zeros((), BF16)
    qs = []
    for h in range(nh):
        q = q_ref[:, h * LANES:(h + 1) * LANES]
        qs.append(jnp.where(lane < HEAD_DIM, q, zero))
        qs.append(jnp.where(lane >= HEAD_DIM, q, zero))
    ones_col = jnp.where(lax.broadcasted_iota(jnp.int32, (tk, LANES), 1) == 0, 1.0, 0.0).astype(BF16)
    acc_ref[...] = jnp.zeros_like(acc_ref)

    def step(j, carry, diagonal):
        start = pl.multiple_of(j * tk, tk)
        new = []
        if diagonal:
            rc = (lax.broadcasted_iota(jnp.int32, (tq, tk), 0) + i * tq) // CHUNK
            cc = (lax.broadcasted_iota(jnp.int32, (tq, tk), 1) + j * tk) // CHUNK
            visible = cc <= rc
        for h in range(nh):
            k = k_ref[pl.ds(start, tk), h * LANES:(h + 1) * LANES]
            v = v_ref[pl.ds(start, tk), h * LANES:(h + 1) * LANES]
            vext = jnp.concatenate([v, ones_col], axis=1)
            for mi in range(2):
                c = 2 * h + mi
                s = lax.dot_general(qs[c], k, _NT, preferred_element_type=F32)
                if diagonal:
                    s = jnp.where(visible, s, NEG_BIG)
                m_old = carry[c]
                m_new = jnp.maximum(m_old, jnp.max(s, axis=-1, keepdims=True))
                alpha = jnp.exp2(m_old - m_new)
                p = jnp.exp2(s - m_new).astype(BF16)
                acc_ref[c] = acc_ref[c] * alpha + jnp.dot(p, vext, preferred_element_type=F32)
                new.append(m_new)
        return tuple(new)

    init = tuple(jnp.full((tq, 1), NEG_BIG, F32) for _ in range(2 * nh))
    nfull = i * (tq // tk)
    carry = lax.fori_loop(0, nfull, lambda j, c: step(j, c, False), init)
    for d in range(tq // tk):
        carry = step(nfull + d, carry, True)
    lam = lam_ref[0]
    for h in range(nh):
        a0 = acc_ref[2 * h]
        a1 = acc_ref[2 * h + 1]
        o = (a0[:, :DIFF_V_DIM] / a0[:, DIFF_V_DIM:DIFF_V_DIM + 1]
             - lam * (a1[:, :DIFF_V_DIM] / a1[:, DIFF_V_DIM:DIFF_V_DIM + 1]))
        ms = jnp.mean(o * o, axis=-1, keepdims=True)
        o = o * lax.rsqrt(ms + NORM_EPS) * g_ref[...] * (1.0 - lam_init)
        o_ref[:, h * LANES:(h + 1) * LANES] = o.astype(o_ref.dtype)


def _diff_attn(lam, qd, kd, vd, sub_g, lam_init, tq=512, tk=512, nh=DIFF_HEADS):
    T = qd.shape[0]
    W = nh * LANES
    whole = lambda h, i: (0, h)
    return pl.pallas_call(
        functools.partial(_diff_kernel, lam_init=lam_init, nh=nh, tk=tk),
        grid=(DIFF_HEADS // nh, T // tq),
        in_specs=[
            pl.BlockSpec(memory_space=pltpu.SMEM),
            pl.BlockSpec((tq, W), lambda h, i: (i, h)),
            pl.BlockSpec((T, W), whole, pipeline_mode=pl.Buffered(1)),
            pl.BlockSpec((T, W), whole, pipeline_mode=pl.Buffered(1)),
            pl.BlockSpec((1, LANES), lambda h, i: (0, 0)),
        ],
        out_specs=pl.BlockSpec((tq, W), lambda h, i: (i, h)),
        out_shape=jax.ShapeDtypeStruct((T, DIFF_HEADS * DIFF_V_DIM), BF16),
        scratch_shapes=[pltpu.VMEM((2 * nh, tq, 2 * LANES), F32)],
        compiler_params=_cparams(56, 2),
        name="diff_attention",
    )(lam, qd, kd, vd, sub_g)


def _from_token_tiles(v):
    v = jnp.swapaxes(v, 0, 1)
    return jnp.concatenate([v[s] for s in range(SUBLANES)], axis=1)


def _to_token_tiles(v):
    return jnp.swapaxes(jnp.stack([v[:, s * LANES:(s + 1) * LANES] for s in range(SUBLANES)], axis=0), 0, 1)


def _layer_norm(y, g, b):
    mu = jnp.mean(y, axis=-1, keepdims=True)
    d = y - mu
    var = jnp.mean(d * d, axis=-1, keepdims=True)
    return d * lax.rsqrt(var + NORM_EPS) * g + b


def _out_proj_kernel(oa_ref, ob_ref, x_ref, wo_ref, g_ref, b_ref, wrh_ref, wrl_ref, x1_ref, x1t_ref, logit_ref):
    half = oa_ref.shape[1]
    mix = jnp.dot(oa_ref[...], wo_ref[:half, :], preferred_element_type=F32)
    mix = mix + jnp.dot(ob_ref[...], wo_ref[half:, :], preferred_element_type=F32)
    x1 = _layer_norm(DEEPNORM_ALPHA * x_ref[...] + mix, g_ref[...], b_ref[...])
    x1_ref[...] = x1
    x1t_ref[...] = _to_token_tiles(x1)
    xh = x1.astype(BF16)
    xl = (x1 - xh.astype(F32)).astype(BF16)
    wh = wrh_ref[...]
    logits = jnp.dot(xh, wh, preferred_element_type=F32)
    logits = logits + jnp.dot(xl, wh, preferred_element_type=F32)
    logits = logits + jnp.dot(xh, wrl_ref[...], preferred_element_type=F32)
    logit_ref[...] = logits


def _out_proj(oa, ob, x2, w_out, g, b, wr_hi, wr_lo, tm=512):
    T = x2.shape[0]
    row = lambda i: (i, 0)
    fix = lambda i: (0, 0)
    return pl.pallas_call(
        _out_proj_kernel,
        grid=(T // tm,),
        in_specs=[
            pl.BlockSpec((tm, 512), row),
            pl.BlockSpec((tm, 512), row),
            pl.BlockSpec((tm, D_MODEL), row),
            pl.BlockSpec((D_MODEL, D_MODEL), fix),
            pl.BlockSpec((1, D_MODEL), fix),
            pl.BlockSpec((1, D_MODEL), fix),
            pl.BlockSpec((D_MODEL, N_EXPERTS), fix),
            pl.BlockSpec((D_MODEL, N_EXPERTS), fix),
        ],
        out_specs=[pl.BlockSpec((tm, D_MODEL), row), pl.BlockSpec((tm, SUBLANES, LANES), lambda i: (i, 0, 0)),
                   pl.BlockSpec((tm, N_EXPERTS), row)],
        out_shape=[jax.ShapeDtypeStruct((T, D_MODEL), F32), jax.ShapeDtypeStruct((T, SUBLANES, LANES), F32),
                   jax.ShapeDtypeStruct((T, N_EXPERTS), F32)],
        compiler_params=_cparams(48),
        name="out_proj_ln1_router",
    )(oa, ob, x2, w_out, g, b, wr_hi, wr_lo)


def _route_kernel(logit_ref, bias_ref, idx_ref, gate_ref, rank_ref, count_ref, run_ref):
    i = pl.program_id(0)
    tm = logit_ref.shape[0]
    scores = 1.0 / (1.0 + jnp.exp(-logit_ref[...]))
    sel = scores + bias_ref[...]
    lane = lax.broadcasted_iota(jnp.int32, (tm, N_EXPERTS), 1)
    lanef = lane.astype(F32)
    grp = lane // GROUP_SIZE
    neg_inf = -jnp.inf

    @pl.when(i == 0)
    def _():
        run_ref[...] = jnp.zeros_like(run_ref)

    def first_argmax(v):
        m = jnp.max(v, axis=-1, keepdims=True)
        idx = jnp.min(jnp.where(v == m, lanef, float(N_EXPERTS)), axis=-1, keepdims=True)
        return m, idx

    gscore = []
    for g in range(N_GROUPS):
        vg = jnp.where(grp == g, sel, neg_inf)
        m1, i1 = first_argmax(vg)
        m2 = jnp.max(jnp.where(lanef == i1, neg_inf, vg), axis=-1, keepdims=True)
        gscore.append(m1 + m2)
    keepf = jnp.zeros((tm, N_EXPERTS), F32)
    for g in range(N_GROUPS):
        ahead_n = jnp.zeros((tm, 1), F32)
        for o in range(N_GROUPS):
            if o == g:
                continue
            ahead = (gscore[o] > gscore[g]) if o > g else (gscore[o] >= gscore[g])
            ahead_n = ahead_n + jnp.where(ahead, 1.0, 0.0)
        keepf = jnp.where(grp == g, jnp.where(ahead_n < float(TOPK_GROUPS), 1.0, 0.0), keepf)
    cand = jnp.where(keepf > 0.5, sel, neg_inf)

    out_lane = lax.broadcasted_iota(jnp.int32, (tm, LANES), 1)
    idx_acc = jnp.zeros((tm, LANES), F32)
    sc_acc = jnp.zeros((tm, LANES), F32)
    hits = []
    chosen = jnp.zeros((tm, N_EXPERTS), F32)
    for k in range(TOP_K):
        _, ik = first_argmax(cand)
        hit = lanef == ik
        hits.append(hit)
        chosen = jnp.where(hit, 1.0, chosen)
        sk = jnp.sum(jnp.where(hit, scores, 0.0), axis=-1, keepdims=True)
        idx_acc = jnp.where(out_lane == k, ik, idx_acc)
        sc_acc = jnp.where(out_lane == k, sk, sc_acc)
        cand = jnp.where(hit, neg_inf, cand)
    total = jnp.sum(sc_acc, axis=-1, keepdims=True)
    idx_ref[...] = idx_acc.astype(jnp.int32)
    gate_ref[...] = sc_acc / total * ROUTED_SCALE

    r = lax.broadcasted_iota(jnp.int32, (tm, tm), 0)
    c = lax.broadcasted_iota(jnp.int32, (tm, tm), 1)
    lower = jnp.where(c < r, 1.0, 0.0).astype(BF16)
    before = jnp.dot(lower, chosen.astype(BF16), preferred_element_type=F32) + run_ref[...]
    rank_acc = jnp.zeros((tm, LANES), F32)
    for k in range(TOP_K):
        rk = jnp.sum(jnp.where(hits[k], before, 0.0), axis=-1, keepdims=True)
        rank_acc = jnp.where(out_lane == k, rk, rank_acc)
    rank_ref[...] = rank_acc.astype(jnp.int32)
    run_ref[...] = run_ref[...] + jnp.sum(chosen, axis=0, keepdims=True)
    count_ref[...] = run_ref[...].astype(jnp.int32)


def _route(logits, r_bias, tm=512):
    T = logits.shape[0]
    row = lambda i: (i, 0)
    fix = lambda i: (0, 0)
    return pl.pallas_call(
        _route_kernel,
        grid=(T // tm,),
        in_specs=[pl.BlockSpec((tm, N_EXPERTS), row), pl.BlockSpec((1, N_EXPERTS), fix)],
        out_specs=[pl.BlockSpec((tm, LANES), row), pl.BlockSpec((tm, LANES), row), pl.BlockSpec((tm, LANES), row),
                   pl.BlockSpec((1, N_EXPERTS), fix)],
        out_shape=[jax.ShapeDtypeStruct((T, LANES), jnp.int32), jax.ShapeDtypeStruct((T, LANES), F32),
                   jax.ShapeDtypeStruct((T, LANES), jnp.int32), jax.ShapeDtypeStruct((1, N_EXPERTS), jnp.int32)],
        scratch_shapes=[pltpu.VMEM((1, N_EXPERTS), F32)],
        compiler_params=_cparams(32),
        name="moe_route",
    )(logits, r_bias)


def _row_pos_kernel(idx_ref, rank_ref, pstart_ref, pos_ref):
    tm = idx_ref.shape[0]
    lanef = lax.broadcasted_iota(jnp.int32, (tm, N_EXPERTS), 1).astype(F32)
    out_lane = lax.broadcasted_iota(jnp.int32, (tm, LANES), 1)
    idx = idx_ref[...].astype(F32)
    pstart = pstart_ref[...]
    base = jnp.zeros((tm, LANES), F32)
    for k in range(TOP_K):
        ik = idx[:, k:k + 1]
        pk = jnp.sum(jnp.where(lanef == ik, pstart, 0.0), axis=-1, keepdims=True)
        base = jnp.where(out_lane == k, pk, base)
    pos_ref[...] = base.astype(jnp.int32) + rank_ref[...]


def _row_pos(idx, rank, pstarts, tm=512):
    T = idx.shape[0]
    row = lambda i: (i, 0)
    return pl.pallas_call(
        _row_pos_kernel,
        grid=(T // tm,),
        in_specs=[pl.BlockSpec((tm, LANES), row), pl.BlockSpec((tm, LANES), row),
                  pl.BlockSpec((1, N_EXPERTS), lambda i: (0, 0))],
        out_specs=pl.BlockSpec((tm, LANES), row),
        out_shape=jax.ShapeDtypeStruct((T, LANES), jnp.int32),
        compiler_params=_cparams(32),
        name="moe_row_pos",
    )(idx, rank, pstarts)


def _start_row_gather(idx_ref, n_rows, src_hbm, dst, row0, sem, unrolled=False, idx0=0):
    ids = idx_ref.at[0, 0, pl.ds(idx0, n_rows)]

    def start(r):
        t = ids[r]
        pltpu.make_async_copy(src_hbm.at[t], dst.at[row0 + r], sem).start()

    if unrolled:
        for r in range(n_rows):
            start(r)
    else:
        def body(r, carry):
            start(r)
            return carry
        lax.fori_loop(0, n_rows, body, 0, unroll=8)


def _wait_row_gather(n_rows, src_hbm, dst, row0, sem):
    pltpu.make_async_copy(src_hbm.at[pl.ds(0, n_rows)], dst.at[pl.ds(row0, n_rows)], sem).wait()


GATHER_SLOTS = 3
BLOCK_GROUP = 8


def _expert_kernel(blk_ref, e_ref, lo_ref, hi_ref, tok0_ref, tok1_ref, tok2_ref, gate_ref, x_hbm,
                   wg_ref, wu_ref, wd_ref, y_ref, buf, sem):
    del e_ref
    i = pl.program_id(0)
    n = pl.num_programs(0)
    rows = MOE_BLOCK

    def in_group(d):
        return (blk_ref[jnp.minimum(i + d, n - 1)] % BLOCK_GROUP) * rows

    @pl.when(i == 0)
    def _():
        _start_row_gather(tok0_ref, rows, x_hbm, buf, 0, sem.at[0], idx0=in_group(0))
        _start_row_gather(tok1_ref, rows, x_hbm, buf, rows, sem.at[1], idx0=in_group(1))

    @pl.when(jnp.logical_or(i == 0, blk_ref[i] != blk_ref[jnp.maximum(i - 1, 0)]))
    def _():
        y_ref[...] = jnp.zeros_like(y_ref)

    slot = i % GATHER_SLOTS
    row0 = pl.multiple_of(slot * rows, rows)
    _wait_row_gather(rows, x_hbm, buf, row0, sem.at[slot])
    xg = _from_token_tiles(buf[pl.ds(row0, rows)]).astype(BF16)
    nslot = (i + 2) % GATHER_SLOTS
    _start_row_gather(tok2_ref, rows, x_hbm, buf, nslot * rows, sem.at[nslot], unrolled=True, idx0=in_group(2))
    a = jnp.dot(xg, wg_ref[...].astype(BF16), preferred_element_type=F32)
    u = jnp.dot(xg, wu_ref[...].astype(BF16), preferred_element_type=F32)
    h = (a / (1.0 + jnp.exp(-a)) * u).astype(BF16)
    y = jnp.dot(h, wd_ref[...].astype(BF16), preferred_element_type=F32)
    r = lax.broadcasted_iota(jnp.int32, (rows, 1), 0)
    mine = jnp.logical_and(r >= lo_ref[i], r < hi_ref[i])
    gate = gate_ref[0, pl.ds(pl.multiple_of(in_group(0), rows), rows), :]
    y_ref[...] += _to_token_tiles(jnp.where(mine, y * gate, 0.0))

    @pl.when(i == n - 1)
    def _():
        for d in (1, 2):
            s = (i + d) % GATHER_SLOTS
            _wait_row_gather(rows, x_hbm, buf, s * rows, sem.at[s])


def _experts(items, tok_blocks, gate_blocks, x1, w_gate, w_up, w_down):
    it_blk, it_e, it_lo, it_hi = items
    NI = it_blk.shape[0]
    NBLK = tok_blocks.shape[0] * BLOCK_GROUP
    ahead = lambda d: (lambda i, blk, e, lo, hi: (blk[jnp.minimum(i + d, NI - 1)] // BLOCK_GROUP, 0, 0))
    wsel = lambda i, blk, e, lo, hi: (e[i], 0, 0)
    group_rows = BLOCK_GROUP * MOE_BLOCK
    grid_spec = pltpu.PrefetchScalarGridSpec(
        num_scalar_prefetch=4,
        grid=(NI,),
        in_specs=[
            pl.BlockSpec((1, 1, group_rows), ahead(0), memory_space=pltpu.SMEM),
            pl.BlockSpec((1, 1, group_rows), ahead(1), memory_space=pltpu.SMEM),
            pl.BlockSpec((1, 1, group_rows), ahead(2), memory_space=pltpu.SMEM),
            pl.BlockSpec((1, group_rows, 1), ahead(0)),
            pl.BlockSpec(memory_space=pl.ANY),
            pl.BlockSpec((None, D_MODEL, EXPERT_FF), wsel),
            pl.BlockSpec((None, D_MODEL, EXPERT_FF), wsel),
            pl.BlockSpec((None, EXPERT_FF, D_MODEL), wsel),
        ],
        out_specs=pl.BlockSpec((MOE_BLOCK, SUBLANES, LANES), lambda i, blk, e, lo, hi: (blk[i], 0, 0)),
        scratch_shapes=[pltpu.VMEM((GATHER_SLOTS * MOE_BLOCK, SUBLANES, LANES), F32),
                        pltpu.SemaphoreType.DMA((GATHER_SLOTS,))],
    )
    return pl.pallas_call(
        _expert_kernel,
        grid_spec=grid_spec,
        out_shape=jax.ShapeDtypeStruct((NBLK * MOE_BLOCK, SUBLANES, LANES), F32),
        compiler_params=_cparams(48),
        name="moe_experts",
    )(it_blk, it_e, it_lo, it_hi, tok_blocks, tok_blocks, tok_blocks, gate_blocks, x1, w_gate, w_up, w_down)


def _combine_kernel(posc_ref, posn_ref, ys_hbm, x1_ref, wsg_ref, wsu_ref, wsd_ref, g_ref, b_ref, out_ref, buf, sem):
    i = pl.program_id(0)
    n = pl.num_programs(0)
    tt = x1_ref.shape[0]
    rows = TOP_K * tt
    slot = i % 2

    @pl.when(i == 0)
    def _():
        _start_row_gather(posc_ref, rows, ys_hbm, buf, 0, sem.at[0])

    @pl.when(i + 1 < n)
    def _():
        nslot = (i + 1) % 2
        _start_row_gather(posn_ref, rows, ys_hbm, buf, nslot * rows, sem.at[nslot])

    x1 = x1_ref[...]
    xb = x1.astype(BF16)
    a = jnp.dot(xb, wsg_ref[...], preferred_element_type=F32)
    u = jnp.dot(xb, wsu_ref[...], preferred_element_type=F32)
    h = (a / (1.0 + jnp.exp(-a)) * u).astype(BF16)
    ffn = jnp.dot(h, wsd_ref[...], preferred_element_type=F32)

    row0 = pl.multiple_of(slot * rows, rows)
    _wait_row_gather(rows, ys_hbm, buf, row0, sem.at[slot])
    routed = buf[pl.ds(row0, tt)]
    for k in range(1, TOP_K):
        routed = routed + buf[pl.ds(row0 + k * tt, tt)]
    ffn = ffn + _from_token_tiles(routed)
    out_ref[...] = _layer_norm(DEEPNORM_ALPHA * x1 + ffn, g_ref[...], b_ref[...])


def _combine(pos_blocks, ys, x1, ws_gate, ws_up, ws_down, g, b, tt=128):
    T = x1.shape[0]
    n = T // tt
    rows = TOP_K * tt
    row = lambda i: (i, 0)
    fix = lambda i: (0, 0)
    return pl.pallas_call(
        _combine_kernel,
        grid=(n,),
        in_specs=[
            pl.BlockSpec((1, 1, rows), lambda i: (i, 0, 0), memory_space=pltpu.SMEM),
            pl.BlockSpec((1, 1, rows), lambda i: (jnp.minimum(i + 1, n - 1), 0, 0), memory_space=pltpu.SMEM),
            pl.BlockSpec(memory_space=pl.ANY),
            pl.BlockSpec((tt, D_MODEL), row),
            pl.BlockSpec((D_MODEL, SHARED_FF), fix),
            pl.BlockSpec((D_MODEL, SHARED_FF), fix),
            pl.BlockSpec((SHARED_FF, D_MODEL), fix),
            pl.BlockSpec((1, D_MODEL), fix),
            pl.BlockSpec((1, D_MODEL), fix),
        ],
        out_specs=pl.BlockSpec((tt, D_MODEL), row),
        out_shape=jax.ShapeDtypeStruct((T, D_MODEL), F32),
        scratch_shapes=[pltpu.VMEM((2 * rows, SUBLANES, LANES), F32), pltpu.SemaphoreType.DMA((2,))],
        compiler_params=_cparams(48),
        name="moe_combine_ln2",
    )(pos_blocks, pos_blocks, ys, x1, ws_gate, ws_up, ws_down, g, b)


def _rope_tables(T):
    inv = 1.0 / (ROPE_THETA ** (jnp.arange(0, HEAD_DIM, 2, dtype=F32) / HEAD_DIM))
    ang = jnp.arange(T, dtype=F32)[:, None] * inv[None, :]
    cos, sin = jnp.cos(ang), jnp.sin(ang)
    cos128 = jnp.tile(cos, (1, 4))
    sin128 = jnp.tile(jnp.concatenate([-sin, sin], axis=1), (1, 2))
    return cos128, sin128


def _swa_head_order(a, axis):
    half = SWA_Q_HEADS // 2
    shp = a.shape
    a = a.reshape(shp[:axis] + (2, half, HEAD_DIM) + shp[axis + 1:])
    return jnp.swapaxes(a, axis, axis + 1).reshape(shp)


def _item_tables(top_e, gate, counts, T):
    A = T * TOP_K
    NBLK = A // MOE_BLOCK
    NI = NBLK + N_EXPERTS - 1
    starts = jnp.cumsum(counts) - counts
    ends = starts + counts
    first_blk = starts // MOE_BLOCK
    nblk = jnp.where(counts > 0, (ends - 1) // MOE_BLOCK - first_blk + 1, 0)
    item_end = jnp.cumsum(nblk)
    item_off = item_end - nblk
    it = jnp.arange(NI, dtype=jnp.int32)
    live = it < item_end[-1]
    it_e = jnp.minimum(jnp.sum((item_end[None, :] <= it[:, None]).astype(jnp.int32), axis=1), N_EXPERTS - 1)
    onehot = (it_e[:, None] == jnp.arange(N_EXPERTS, dtype=jnp.int32)[None, :]).astype(jnp.int32)
    pick = lambda v: jnp.sum(onehot * v[None, :].astype(jnp.int32), axis=1)
    it_blk = pick(first_blk) + it - pick(item_off)
    it_lo = jnp.clip(pick(starts) - it_blk * MOE_BLOCK, 0, MOE_BLOCK)
    it_hi = jnp.clip(pick(ends) - it_blk * MOE_BLOCK, 0, MOE_BLOCK)
    it_blk = jnp.where(live, it_blk, NBLK - 1)
    it_lo = jnp.where(live, it_lo, 0)
    it_hi = jnp.where(live, it_hi, 0)
    tok = jnp.arange(T, dtype=jnp.int32)[:, None]
    key_s, gate_s = lax.sort(((top_e * T + tok).reshape(A), gate.reshape(A)), num_keys=1)
    items = tuple(v.astype(jnp.int32) for v in (it_blk, it_e, it_lo, it_hi))
    group_rows = BLOCK_GROUP * MOE_BLOCK
    return (items, (key_s % T).reshape(A // group_rows, 1, group_rows),
            gate_s.reshape(A // group_rows, group_rows, 1), starts)


def kernel(x, w_in, b_in, sinks, lam_q1, lam_k1, lam_q2, lam_k2, diff_norm_g, w_out, ln1_g, ln1_b,
           w_router, router_bias, w_gate, w_up, w_down, ws_gate, ws_up, ws_down, ln2_g, ln2_b):
    B, S, D = x.shape
    assert D == D_MODEL and w_in.shape[0] == DEPTH
    T = B * S
    assert B == 1 and T % 512 == 0
    cos128, sin128 = _rope_tables(S)
    x2 = x.reshape(T, D)
    for l in range(DEPTH):
        lam_init = 0.8 - 0.6 * math.exp(-0.3 * l)
        wi = w_in[l]
        wi = jnp.concatenate([_swa_head_order(wi[:, :_AK], 1), wi[:, _AK:]], axis=1).astype(BF16)
        bi = jnp.concatenate([_swa_head_order(b_in[l][:_AK], 0), b_in[l][_AK:]])[None, :]
        wo = jnp.concatenate([_swa_head_order(w_out[l][:_AK], 0), w_out[l][_AK:]], axis=0).astype(BF16)
        sink_p = sinks[l].astype(F32)
        lam = (jnp.exp(jnp.sum(lam_q1[l].astype(F32) * lam_k1[l].astype(F32)))
               - jnp.exp(jnp.sum(lam_q2[l].astype(F32) * lam_k2[l].astype(F32))) + lam_init).reshape(1)
        wr = w_router[l]
        wr_hi = wr.astype(BF16)
        wr_lo = (wr - wr_hi.astype(F32)).astype(BF16)

        qa, ka, va, qd, kd, vd = _in_proj(x2, wi, bi, cos128, sin128)
        oa = _swa(sink_p, qa, ka, va)
        ob = _diff_attn(lam, qd, kd, vd, diff_norm_g[l][None, :], lam_init)
        x1, x1t, logits = _out_proj(oa, ob, x2, wo, ln1_g[l][None, :], ln1_b[l][None, :], wr_hi, wr_lo)
        idx, gates, rank, counts = _route(logits, router_bias[l][None, :])
        items, tok_blocks, gate_blocks, starts = _item_tables(idx[:, :TOP_K], gates[:, :TOP_K], counts[0], T)
        pos = _row_pos(idx, rank, starts.astype(F32)[None, :])[:, :TOP_K]
        ys = _experts(items, tok_blocks, gate_blocks, x1t, w_gate[l], w_up[l], w_down[l])
        tt = 128
        pos_blocks = pos.reshape(T // tt, tt, TOP_K).transpose(0, 2, 1).reshape(T // tt, 1, TOP_K * tt)
        x2 = _combine(pos_blocks, ys, x1, ws_gate[l].astype(BF16), ws_up[l].astype(BF16),
                      ws_down[l].astype(BF16), ln2_g[l][None, :], ln2_b[l][None, :], tt=tt)
    return x2.reshape(B, S, D)
```

```python
import functools
import math

import numpy as np
import jax
import jax.numpy as jnp
from jax import lax
from jax.experimental import pallas as pl
from jax.experimental.pallas import tpu as pltpu

D_MODEL = 1024
CHUNK = 64
HEAD_DIM = 64
ROPE_THETA = 10000.0
SWA_Q_HEADS = 8
SWA_KV_HEADS = 2
SWA_WIN_CHUNKS = 2
DIFF_HEADS = 4
DIFF_V_DIM = 2 * HEAD_DIM
N_EXPERTS = 256
TOP_K = 8
N_GROUPS = 8
TOPK_GROUPS = 4
GROUP_SIZE = N_EXPERTS // N_GROUPS
EXPERT_FF = 256
SHARED_FF = 256
ROUTED_SCALE = 2.5
MOE_BLOCK = 128
DEPTH = 1
DEEPNORM_ALPHA = (2.0 * DEPTH) ** 0.25
NORM_EPS = 1e-5
LOG2E = math.log2(math.e)
NEG_BIG = -1e30
LANES = 128
SUBLANES = 8

_AQ, _AK, _AV, _BQ, _BK, _BV, _END = 0, 512, 640, 768, 1280, 1792, 2304
IN_WIDTH = _END

_NT = (((1,), (1,)), ((), ()))

F32 = jnp.float32
BF16 = jnp.bfloat16


def _cparams(vmem_mb, n_axes=1):
    return pltpu.CompilerParams(dimension_semantics=("arbitrary",) * n_axes,
                                vmem_limit_bytes=vmem_mb * 1024 * 1024)


def _in_proj_kernel(x_ref, w_ref, b_ref, cos_ref, sin_ref,
                    qa_ref, ka_ref, va_ref, qd_ref, kd_ref, vd_ref):
    tm = x_ref.shape[0]
    x = x_ref[...].astype(BF16)
    cos = cos_ref[...]
    sin = sin_ref[...]
    lane = lax.broadcasted_iota(jnp.int32, (tm, LANES), 1)
    first_half = (lane % HEAD_DIM) < (HEAD_DIM // 2)
    q_scale = HEAD_DIM ** -0.5 * LOG2E

    def proj(c0, c1):
        return jnp.dot(x, w_ref[:, c0:c1], preferred_element_type=F32) + b_ref[:, c0:c1]

    def rope(c):
        rot = jnp.where(first_half, pltpu.roll(c, LANES - 32, 1), pltpu.roll(c, 32, 1))
        return c * cos + rot * sin

    def emit(out_ref, c0, c1, rotary, scale):
        for g0 in range(c0, c1, 256):
            g1 = min(g0 + 256, c1)
            p = proj(g0, g1)
            for j in range(0, g1 - g0, LANES):
                c = p[:, j:j + LANES]
                if rotary:
                    c = rope(c)
                if scale != 1.0:
                    c = c * scale
                out_ref[:, g0 - c0 + j:g0 - c0 + j + LANES] = c.astype(out_ref.dtype)

    emit(qa_ref, _AQ, _AK, True, q_scale)
    emit(ka_ref, _AK, _AV, True, 1.0)
    emit(va_ref, _AV, _BQ, False, 1.0)
    emit(qd_ref, _BQ, _BK, True, q_scale)
    emit(kd_ref, _BK, _BV, True, 1.0)
    emit(vd_ref, _BV, _END, False, 1.0)


def _in_proj(x2, w_in, b_in, cos128, sin128, tm=512):
    T = x2.shape[0]
    outs = [(T, 512), (T, 128), (T, 128), (T, 512), (T, 512), (T, 512)]
    return pl.pallas_call(
        _in_proj_kernel,
        grid=(T // tm,),
        in_specs=[
            pl.BlockSpec((tm, D_MODEL), lambda i: (i, 0)),
            pl.BlockSpec((D_MODEL, IN_WIDTH), lambda i: (0, 0)),
            pl.BlockSpec((1, IN_WIDTH), lambda i: (0, 0)),
            pl.BlockSpec((tm, LANES), lambda i: (i, 0)),
            pl.BlockSpec((tm, LANES), lambda i: (i, 0)),
        ],
        out_specs=[pl.BlockSpec((tm, s[1]), lambda i: (i, 0)) for s in outs],
        out_shape=[jax.ShapeDtypeStruct(s, BF16) for s in outs],
        compiler_params=_cparams(48),
        name="in_proj_rope",
    )(x2, w_in, b_in, cos128, sin128)


def _swa_kernel(sink_ref, q_ref, kp_ref, kc_ref, vp_ref, vc_ref, o_ref):
    i = pl.program_id(0)
    tq = q_ref.shape[0]
    back = SWA_WIN_CHUNKS * CHUNK
    keys = jnp.concatenate([kp_ref[tq - back:, :], kc_ref[...]], axis=0)
    vals = jnp.concatenate([vp_ref[tq - back:, :], vc_ref[...]], axis=0)
    nk = tq + back
    qc = lax.broadcasted_iota(jnp.int32, (tq, nk), 0) // CHUNK
    kc = lax.broadcasted_iota(jnp.int32, (tq, nk), 1) // CHUNK - SWA_WIN_CHUNKS
    lo = jnp.where(i > 0, qc - SWA_WIN_CHUNKS, jnp.maximum(qc - SWA_WIN_CHUNKS, 0))
    bias = jnp.where(kc <= qc, jnp.where(kc >= lo, 0.0, NEG_BIG), NEG_BIG).astype(F32)
    lane = lax.broadcasted_iota(jnp.int32, (tq, LANES), 1)
    low = lane < HEAD_DIM
    zero = jnp.zeros((), BF16)
    for c in range(SWA_Q_HEADS // 2):
        qch = q_ref[:, c * LANES:(c + 1) * LANES]
        outs = []
        for half in range(2):
            h = c + (SWA_Q_HEADS // 2) * half
            qm = jnp.where(low if half == 0 else jnp.logical_not(low), qch, zero)
            s = lax.dot_general(qm, keys, _NT, preferred_element_type=F32) + bias
            sink = sink_ref[h] * LOG2E
            m = jnp.maximum(jnp.max(s, axis=-1, keepdims=True), sink)
            p = jnp.exp2(s - m)
            denom = jnp.sum(p, axis=-1, keepdims=True) + jnp.exp2(sink - m)
            o = jnp.dot(p.astype(BF16), vals, preferred_element_type=F32)
            outs.append(o / denom)
        o_ref[:, c * LANES:(c + 1) * LANES] = jnp.where(low, outs[0], outs[1]).astype(o_ref.dtype)


def _swa(sinks, qa, ka, va, tq=256):
    T = qa.shape[0]
    prev = lambda i: (jnp.maximum(i - 1, 0), 0)
    cur = lambda i: (i, 0)
    return pl.pallas_call(
        _swa_kernel,
        grid=(T // tq,),
        in_specs=[
            pl.BlockSpec(memory_space=pltpu.SMEM),
            pl.BlockSpec((tq, 512), cur),
            pl.BlockSpec((tq, LANES), prev),
            pl.BlockSpec((tq, LANES), cur),
            pl.BlockSpec((tq, LANES), prev),
            pl.BlockSpec((tq, LANES), cur),
        ],
        out_specs=pl.BlockSpec((tq, 512), cur),
        out_shape=jax.ShapeDtypeStruct((T, 512), BF16),
        compiler_params=_cparams(32),
        name="swa_sink_attention",
    )(sinks, qa, ka, ka, va, va)


def _diff_kernel(lam_ref, q_ref, k_ref, v_ref, g_ref, o_ref, acc_ref, *, lam_init, nh, tk):
    i = pl.program_id(1)
    tq = q_ref.shape[0]
    lane = lax.broadcasted_iota(jnp.int32, (tq, LANES), 1)
    zero = jnp.zeros((), BF16)
    qs = []
    for h in range(nh):
        q = q_ref[:, h * LANES:(h + 1) * LANES]
        qs.append(jnp.where(lane < HEAD_DIM, q, zero))
        qs.append(jnp.where(lane >= HEAD_DIM, q, zero))
    ones_col = jnp.where(lax.broadcasted_iota(jnp.int32, (tk, LANES), 1) == 0, 1.0, 0.0).astype(BF16)
    acc_ref[...] = jnp.zeros_like(acc_ref)

    def step(j, carry, diagonal):
        start = pl.multiple_of(j * tk, tk)
        new = []
        if diagonal:
            rc = (lax.broadcasted_iota(jnp.int32, (tq, tk), 0) + i * tq) // CHUNK
            cc = (lax.broadcasted_iota(jnp.int32, (tq, tk), 1) + j * tk) // CHUNK
            visible = cc <= rc
        for h in range(nh):
            k = k_ref[pl.ds(start, tk), h * LANES:(h + 1) * LANES]
            v = v_ref[pl.ds(start, tk), h * LANES:(h + 1) * LANES]
            vext = jnp.concatenate([v, ones_col], axis=1)
            for mi in range(2):
                c = 2 * h + mi
                s = lax.dot_general(qs[c], k, _NT, preferred_element_type=F32)
                if diagonal:
                    s = jnp.where(visible, s, NEG_BIG)
                m_old = carry[c]
                m_new = jnp.maximum(m_old, jnp.max(s, axis=-1, keepdims=True))
                alpha = jnp.exp2(m_old - m_new)
                p = jnp.exp2((s - m_new).astype(BF16))
                acc_ref[c] = acc_ref[c] * alpha + jnp.dot(p, vext, preferred_element_type=F32)
                new.append(m_new)
        return tuple(new)

    init = tuple(jnp.full((tq, 1), NEG_BIG, F32) for _ in range(2 * nh))
    nfull = i * (tq // tk)
    carry = lax.fori_loop(0, nfull, lambda j, c: step(j, c, False), init)
    for d in range(tq // tk):
        carry = step(nfull + d, carry, True)
    lam = lam_ref[0]
    for h in range(nh):
        a0 = acc_ref[2 * h]
        a1 = acc_ref[2 * h + 1]
        o = (a0[:, :DIFF_V_DIM] / a0[:, DIFF_V_DIM:DIFF_V_DIM + 1]
             - lam * (a1[:, :DIFF_V_DIM] / a1[:, DIFF_V_DIM:DIFF_V_DIM + 1]))
        ms = jnp.mean(o * o, axis=-1, keepdims=True)
        o = o * lax.rsqrt(ms + NORM_EPS) * g_ref[...] * (1.0 - lam_init)
        o_ref[:, h * LANES:(h + 1) * LANES] = o.astype(o_ref.dtype)


def _diff_attn(lam, qd, kd, vd, sub_g, lam_init, tq=512, tk=512, nh=DIFF_HEADS):
    T = qd.shape[0]
    W = nh * LANES
    whole = lambda h, i: (0, h)
    return pl.pallas_call(
        functools.partial(_diff_kernel, lam_init=lam_init, nh=nh, tk=tk),
        grid=(DIFF_HEADS // nh, T // tq),
        in_specs=[
            pl.BlockSpec(memory_space=pltpu.SMEM),
            pl.BlockSpec((tq, W), lambda h, i: (i, h)),
            pl.BlockSpec((T, W), whole, pipeline_mode=pl.Buffered(1)),
            pl.BlockSpec((T, W), whole, pipeline_mode=pl.Buffered(1)),
            pl.BlockSpec((1, LANES), lambda h, i: (0, 0)),
        ],
        out_specs=pl.BlockSpec((tq, W), lambda h, i: (i, h)),
        out_shape=jax.ShapeDtypeStruct((T, DIFF_HEADS * DIFF_V_DIM), BF16),
        scratch_shapes=[pltpu.VMEM((2 * nh, tq, 2 * LANES), F32)],
        compiler_params=_cparams(56, 2),
        name="diff_attention",
    )(lam, qd, kd, vd, sub_g)


def _from_token_tiles(v):
    v = jnp.swapaxes(v, 0, 1)
    return jnp.concatenate([v[s] for s in range(SUBLANES)], axis=1)


def _to_token_tiles(v):
    return jnp.swapaxes(jnp.stack([v[:, s * LANES:(s + 1) * LANES] for s in range(SUBLANES)], axis=0), 0, 1)


def _layer_norm(y, g, b):
    mu = jnp.mean(y, axis=-1, keepdims=True)
    d = y - mu
    var = jnp.mean(d * d, axis=-1, keepdims=True)
    return d * lax.rsqrt(var + NORM_EPS) * g + b


def _out_proj_kernel(oa_ref, ob_ref, x_ref, wo_ref, g_ref, b_ref, wrh_ref, wrl_ref, x1_ref, x1t_ref, logit_ref):
    half = oa_ref.shape[1]
    mix = jnp.dot(oa_ref[...], wo_ref[:half, :], preferred_element_type=F32)
    mix = mix + jnp.dot(ob_ref[...], wo_ref[half:, :], preferred_element_type=F32)
    x1 = _layer_norm(DEEPNORM_ALPHA * x_ref[...] + mix, g_ref[...], b_ref[...])
    x1_ref[...] = x1
    x1t_ref[...] = _to_token_tiles(x1)
    xh = x1.astype(BF16)
    xl = (x1 - xh.astype(F32)).astype(BF16)
    wh = wrh_ref[...]
    logits = jnp.dot(xh, wh, preferred_element_type=F32)
    logits = logits + jnp.dot(xl, wh, preferred_element_type=F32)
    logits = logits + jnp.dot(xh, wrl_ref[...], preferred_element_type=F32)
    logit_ref[...] = logits


def _out_proj(oa, ob, x2, w_out, g, b, wr_hi, wr_lo, tm=512):
    T = x2.shape[0]
    row = lambda i: (i, 0)
    fix = lambda i: (0, 0)
    return pl.pallas_call(
        _out_proj_kernel,
        grid=(T // tm,),
        in_specs=[
            pl.BlockSpec((tm, 512), row),
            pl.BlockSpec((tm, 512), row),
            pl.BlockSpec((tm, D_MODEL), row),
            pl.BlockSpec((D_MODEL, D_MODEL), fix),
            pl.BlockSpec((1, D_MODEL), fix),
            pl.BlockSpec((1, D_MODEL), fix),
            pl.BlockSpec((D_MODEL, N_EXPERTS), fix),
            pl.BlockSpec((D_MODEL, N_EXPERTS), fix),
        ],
        out_specs=[pl.BlockSpec((tm, D_MODEL), row), pl.BlockSpec((tm, SUBLANES, LANES), lambda i: (i, 0, 0)),
                   pl.BlockSpec((tm, N_EXPERTS), row)],
        out_shape=[jax.ShapeDtypeStruct((T, D_MODEL), F32), jax.ShapeDtypeStruct((T, SUBLANES, LANES), F32),
                   jax.ShapeDtypeStruct((T, N_EXPERTS), F32)],
        compiler_params=_cparams(48),
        name="out_proj_ln1_router",
    )(oa, ob, x2, w_out, g, b, wr_hi, wr_lo)


def _route_kernel(logit_ref, bias_ref, idx_ref, gate_ref, rank_ref, count_ref, run_ref):
    i = pl.program_id(0)
    tm = logit_ref.shape[0]
    scores = 1.0 / (1.0 + jnp.exp(-logit_ref[...]))
    sel = scores + bias_ref[...]
    lane = lax.broadcasted_iota(jnp.int32, (tm, N_EXPERTS), 1)
    lanef = lane.astype(F32)
    grp = lane // GROUP_SIZE
    neg_inf = -jnp.inf

    @pl.when(i == 0)
    def _():
        run_ref[...] = jnp.zeros_like(run_ref)

    def first_argmax(v):
        m = jnp.max(v, axis=-1, keepdims=True)
        idx = jnp.min(jnp.where(v == m, lanef, float(N_EXPERTS)), axis=-1, keepdims=True)
        return m, idx

    gscore = []
    for g in range(N_GROUPS):
        vg = jnp.where(grp == g, sel, neg_inf)
        m1, i1 = first_argmax(vg)
        m2 = jnp.max(jnp.where(lanef == i1, neg_inf, vg), axis=-1, keepdims=True)
        gscore.append(m1 + m2)
    keepf = jnp.zeros((tm, N_EXPERTS), F32)
    for g in range(N_GROUPS):
        ahead_n = jnp.zeros((tm, 1), F32)
        for o in range(N_GROUPS):
            if o == g:
                continue
            ahead = (gscore[o] > gscore[g]) if o > g else (gscore[o] >= gscore[g])
            ahead_n = ahead_n + jnp.where(ahead, 1.0, 0.0)
        keepf = jnp.where(grp == g, jnp.where(ahead_n < float(TOPK_GROUPS), 1.0, 0.0), keepf)
    cand = jnp.where(keepf > 0.5, sel, neg_inf)

    out_lane = lax.broadcasted_iota(jnp.int32, (tm, LANES), 1)
    idx_acc = jnp.zeros((tm, LANES), F32)
    sc_acc = jnp.zeros((tm, LANES), F32)
    hits = []
    chosen = jnp.zeros((tm, N_EXPERTS), F32)
    for k in range(TOP_K):
        _, ik = first_argmax(cand)
        hit = lanef == ik
        hits.append(hit)
        chosen = jnp.where(hit, 1.0, chosen)
        sk = jnp.sum(jnp.where(hit, scores, 0.0), axis=-1, keepdims=True)
        idx_acc = jnp.where(out_lane == k, ik, idx_acc)
        sc_acc = jnp.where(out_lane == k, sk, sc_acc)
        cand = jnp.where(hit, neg_inf, cand)
    total = jnp.sum(sc_acc, axis=-1, keepdims=True)
    idx_ref[...] = idx_acc.astype(jnp.int32)
    gate_ref[...] = sc_acc / total * ROUTED_SCALE

    r = lax.broadcasted_iota(jnp.int32, (tm, tm), 0)
    c = lax.broadcasted_iota(jnp.int32, (tm, tm), 1)
    lower = jnp.where(c < r, 1.0, 0.0).astype(BF16)
    before = jnp.dot(lower, chosen.astype(BF16), preferred_element_type=F32) + run_ref[...]
    rank_acc = jnp.zeros((tm, LANES), F32)
    for k in range(TOP_K):
        rk = jnp.sum(jnp.where(hits[k], before, 0.0), axis=-1, keepdims=True)
        rank_acc = jnp.where(out_lane == k, rk, rank_acc)
    rank_ref[...] = rank_acc.astype(jnp.int32)
    run_ref[...] = run_ref[...] + jnp.sum(chosen, axis=0, keepdims=True)
    count_ref[...] = run_ref[...].astype(jnp.int32)


def _route(logits, r_bias, tm=512):
    T = logits.shape[0]
    row = lambda i: (i, 0)
    fix = lambda i: (0, 0)
    return pl.pallas_call(
        _route_kernel,
        grid=(T // tm,),
        in_specs=[pl.BlockSpec((tm, N_EXPERTS), row), pl.BlockSpec((1, N_EXPERTS), fix)],
        out_specs=[pl.BlockSpec((tm, LANES), row), pl.BlockSpec((tm, LANES), row), pl.BlockSpec((tm, LANES), row),
                   pl.BlockSpec((1, N_EXPERTS), fix)],
        out_shape=[jax.ShapeDtypeStruct((T, LANES), jnp.int32), jax.ShapeDtypeStruct((T, LANES), F32),
                   jax.ShapeDtypeStruct((T, LANES), jnp.int32), jax.ShapeDtypeStruct((1, N_EXPERTS), jnp.int32)],
        scratch_shapes=[pltpu.VMEM((1, N_EXPERTS), F32)],
        compiler_params=_cparams(32),
        name="moe_route",
    )(logits, r_bias)


def _row_pos_kernel(idx_ref, rank_ref, pstart_ref, pos_ref):
    tm = idx_ref.shape[0]
    lanef = lax.broadcasted_iota(jnp.int32, (tm, N_EXPERTS), 1).astype(F32)
    out_lane = lax.broadcasted_iota(jnp.int32, (tm, LANES), 1)
    idx = idx_ref[...].astype(F32)
    pstart = pstart_ref[...]
    base = jnp.zeros((tm, LANES), F32)
    for k in range(TOP_K):
        ik = idx[:, k:k + 1]
        pk = jnp.sum(jnp.where(lanef == ik, pstart, 0.0), axis=-1, keepdims=True)
        base = jnp.where(out_lane == k, pk, base)
    pos_ref[...] = base.astype(jnp.int32) + rank_ref[...]


def _row_pos(idx, rank, pstarts, tm=512):
    T = idx.shape[0]
    row = lambda i: (i, 0)
    return pl.pallas_call(
        _row_pos_kernel,
        grid=(T // tm,),
        in_specs=[pl.BlockSpec((tm, LANES), row), pl.BlockSpec((tm, LANES), row),
                  pl.BlockSpec((1, N_EXPERTS), lambda i: (0, 0))],
        out_specs=pl.BlockSpec((tm, LANES), row),
        out_shape=jax.ShapeDtypeStruct((T, LANES), jnp.int32),
        compiler_params=_cparams(32),
        name="moe_row_pos",
    )(idx, rank, pstarts)


GATHER_UNROLL = 8


def _start_row_gather(idx_ref, n_rows, src_hbm, dst, row0, sem, unrolled=False, idx0=0):
    ids = idx_ref.at[0, 0, pl.ds(idx0, n_rows)]

    def start(r, priority):
        t = ids[r]
        pltpu.make_async_copy(src_hbm.at[t], dst.at[row0 + r], sem).start(priority=priority)

    if unrolled:
        for r in range(n_rows):
            start(r, r % 2)
    else:
        def body(g, carry):
            for j in range(GATHER_UNROLL):
                start(g * GATHER_UNROLL + j, j % 2)
            return carry
        lax.fori_loop(0, n_rows // GATHER_UNROLL, body, 0)


def _wait_row_gather(n_rows, src_hbm, dst, row0, sem):
    pltpu.make_async_copy(src_hbm.at[pl.ds(0, n_rows)], dst.at[pl.ds(row0, n_rows)], sem).wait()


GATHER_SLOTS = 3
BLOCK_GROUP = 8


def _expert_kernel(blk_ref, e_ref, lo_ref, hi_ref, tok0_ref, tok1_ref, tok2_ref, gate_ref, x_hbm,
                   wg_ref, wu_ref, wd_ref, y_ref, buf, sem):
    del e_ref
    i = pl.program_id(0)
    n = pl.num_programs(0)
    rows = MOE_BLOCK

    def in_group(d):
        return (blk_ref[jnp.minimum(i + d, n - 1)] % BLOCK_GROUP) * rows

    @pl.when(i == 0)
    def _():
        _start_row_gather(tok0_ref, rows, x_hbm, buf, 0, sem.at[0], idx0=in_group(0))
        _start_row_gather(tok1_ref, rows, x_hbm, buf, rows, sem.at[1], idx0=in_group(1))

    @pl.when(jnp.logical_or(i == 0, blk_ref[i] != blk_ref[jnp.maximum(i - 1, 0)]))
    def _():
        y_ref[...] = jnp.zeros_like(y_ref)

    slot = i % GATHER_SLOTS
    row0 = pl.multiple_of(slot * rows, rows)
    _wait_row_gather(rows, x_hbm, buf, row0, sem.at[slot])
    xg = _from_token_tiles(buf[pl.ds(row0, rows)]).astype(BF16)
    nslot = (i + 2) % GATHER_SLOTS
    _start_row_gather(tok2_ref, rows, x_hbm, buf, nslot * rows, sem.at[nslot], unrolled=True, idx0=in_group(2))
    a = jnp.dot(xg, wg_ref[...].astype(BF16), preferred_element_type=F32)
    u = jnp.dot(xg, wu_ref[...].astype(BF16), preferred_element_type=F32)
    h = (a / (1.0 + jnp.exp(-a)) * u).astype(BF16)
    y = jnp.dot(h, wd_ref[...].astype(BF16), preferred_element_type=F32)
    r = lax.broadcasted_iota(jnp.int32, (rows, 1), 0)
    mine = jnp.logical_and(r >= lo_ref[i], r < hi_ref[i])
    grow = gate_ref[0, pl.ds(blk_ref[i] % BLOCK_GROUP, 1), :]
    eye = lax.broadcasted_iota(jnp.int32, (rows, rows), 0) == lax.broadcasted_iota(jnp.int32, (rows, rows), 1)
    gate = jnp.sum(jnp.where(eye, grow, 0.0), axis=1, keepdims=True)
    y_ref[...] += _to_token_tiles(jnp.where(mine, y * gate, 0.0))

    @pl.when(i == n - 1)
    def _():
        for d in (1, 2):
            s = (i + d) % GATHER_SLOTS
            _wait_row_gather(rows, x_hbm, buf, s * rows, sem.at[s])


def _experts(items, tok_blocks, gate_blocks, x1, w_gate, w_up, w_down):
    it_blk, it_e, it_lo, it_hi = items
    NI = it_blk.shape[0]
    NBLK = tok_blocks.shape[0] * BLOCK_GROUP
    ahead = lambda d: (lambda i, blk, e, lo, hi: (blk[jnp.minimum(i + d, NI - 1)] // BLOCK_GROUP, 0, 0))
    wsel = lambda i, blk, e, lo, hi: (e[i], 0, 0)
    group_rows = BLOCK_GROUP * MOE_BLOCK
    grid_spec = pltpu.PrefetchScalarGridSpec(
        num_scalar_prefetch=4,
        grid=(NI,),
        in_specs=[
            pl.BlockSpec((1, 1, group_rows), ahead(0), memory_space=pltpu.SMEM),
            pl.BlockSpec((1, 1, group_rows), ahead(1), memory_space=pltpu.SMEM),
            pl.BlockSpec((1, 1, group_rows), ahead(2), memory_space=pltpu.SMEM),
            pl.BlockSpec((1, BLOCK_GROUP, MOE_BLOCK), ahead(0)),
            pl.BlockSpec(memory_space=pl.ANY),
            pl.BlockSpec((None, D_MODEL, EXPERT_FF), wsel),
            pl.BlockSpec((None, D_MODEL, EXPERT_FF), wsel),
            pl.BlockSpec((None, EXPERT_FF, D_MODEL), wsel),
        ],
        out_specs=pl.BlockSpec((MOE_BLOCK, SUBLANES, LANES), lambda i, blk, e, lo, hi: (blk[i], 0, 0)),
        scratch_shapes=[pltpu.VMEM((GATHER_SLOTS * MOE_BLOCK, SUBLANES, LANES), F32),
                        pltpu.SemaphoreType.DMA((GATHER_SLOTS,))],
    )
    return pl.pallas_call(
        _expert_kernel,
        grid_spec=grid_spec,
        out_shape=jax.ShapeDtypeStruct((NBLK * MOE_BLOCK, SUBLANES, LANES), F32),
        compiler_params=_cparams(48),
        name="moe_experts",
    )(it_blk, it_e, it_lo, it_hi, tok_blocks, tok_blocks, tok_blocks, gate_blocks, x1, w_gate, w_up, w_down)


def _combine_kernel(posc_ref, posn_ref, ys_hbm, x1_ref, wsg_ref, wsu_ref, wsd_ref, g_ref, b_ref, out_ref, buf, sem):
    i = pl.program_id(0)
    n = pl.num_programs(0)
    tt = x1_ref.shape[0]
    rows = TOP_K * tt
    slot = i % 2

    @pl.when(i == 0)
    def _():
        _start_row_gather(posc_ref, rows, ys_hbm, buf, 0, sem.at[0])

    @pl.when(i + 1 < n)
    def _():
        nslot = (i + 1) % 2
        _start_row_gather(posn_ref, rows, ys_hbm, buf, nslot * rows, sem.at[nslot])

    x1 = x1_ref[...]
    xb = x1.astype(BF16)
    a = jnp.dot(xb, wsg_ref[...], preferred_element_type=F32)
    u = jnp.dot(xb, wsu_ref[...], preferred_element_type=F32)
    h = (a / (1.0 + jnp.exp(-a)) * u).astype(BF16)
    ffn = jnp.dot(h, wsd_ref[...], preferred_element_type=F32)

    row0 = pl.multiple_of(slot * rows, rows)
    _wait_row_gather(rows, ys_hbm, buf, row0, sem.at[slot])
    routed = buf[pl.ds(row0, tt)]
    for k in range(1, TOP_K):
        routed = routed + buf[pl.ds(row0 + k * tt, tt)]
    ffn = ffn + _from_token_tiles(routed)
    out_ref[...] = _layer_norm(DEEPNORM_ALPHA * x1 + ffn, g_ref[...], b_ref[...])


def _combine(pos_blocks, ys, x1, ws_gate, ws_up, ws_down, g, b, tt=128):
    T = x1.shape[0]
    n = T // tt
    rows = TOP_K * tt
    row = lambda i: (i, 0)
    fix = lambda i: (0, 0)
    return pl.pallas_call(
        _combine_kernel,
        grid=(n,),
        in_specs=[
            pl.BlockSpec((1, 1, rows), lambda i: (i, 0, 0), memory_space=pltpu.SMEM),
            pl.BlockSpec((1, 1, rows), lambda i: (jnp.minimum(i + 1, n - 1), 0, 0), memory_space=pltpu.SMEM),
            pl.BlockSpec(memory_space=pl.ANY),
            pl.BlockSpec((tt, D_MODEL), row),
            pl.BlockSpec((D_MODEL, SHARED_FF), fix),
            pl.BlockSpec((D_MODEL, SHARED_FF), fix),
            pl.BlockSpec((SHARED_FF, D_MODEL), fix),
            pl.BlockSpec((1, D_MODEL), fix),
            pl.BlockSpec((1, D_MODEL), fix),
        ],
        out_specs=pl.BlockSpec((tt, D_MODEL), row),
        out_shape=jax.ShapeDtypeStruct((T, D_MODEL), F32),
        scratch_shapes=[pltpu.VMEM((2 * rows, SUBLANES, LANES), F32), pltpu.SemaphoreType.DMA((2,))],
        compiler_params=_cparams(48),
        name="moe_combine_ln2",
    )(pos_blocks, pos_blocks, ys, x1, ws_gate, ws_up, ws_down, g, b)


def _rope_tables(T):
    inv = 1.0 / (ROPE_THETA ** (jnp.arange(0, HEAD_DIM, 2, dtype=F32) / HEAD_DIM))
    ang = jnp.arange(T, dtype=F32)[:, None] * inv[None, :]
    cos, sin = jnp.cos(ang), jnp.sin(ang)
    cos128 = jnp.tile(cos, (1, 4))
    sin128 = jnp.tile(jnp.concatenate([-sin, sin], axis=1), (1, 2))
    return cos128, sin128


def _swa_head_order(a, axis):
    half = SWA_Q_HEADS // 2
    shp = a.shape
    a = a.reshape(shp[:axis] + (2, half, HEAD_DIM) + shp[axis + 1:])
    return jnp.swapaxes(a, axis, axis + 1).reshape(shp)


def _item_tables(top_e, gate, counts, T):
    A = T * TOP_K
    NBLK = A // MOE_BLOCK
    NI = NBLK + N_EXPERTS - 1
    starts = jnp.cumsum(counts) - counts
    ends = starts + counts
    first_blk = starts // MOE_BLOCK
    nblk = jnp.where(counts > 0, (ends - 1) // MOE_BLOCK - first_blk + 1, 0)
    item_end = jnp.cumsum(nblk)
    item_off = item_end - nblk
    it = jnp.arange(NI, dtype=jnp.int32)
    live = it < item_end[-1]
    it_e = jnp.minimum(jnp.sum((item_end[None, :] <= it[:, None]).astype(jnp.int32), axis=1), N_EXPERTS - 1)
    onehot = (it_e[:, None] == jnp.arange(N_EXPERTS, dtype=jnp.int32)[None, :]).astype(jnp.int32)
    pick = lambda v: jnp.sum(onehot * v[None, :].astype(jnp.int32), axis=1)
    it_blk = pick(first_blk) + it - pick(item_off)
    it_lo = jnp.clip(pick(starts) - it_blk * MOE_BLOCK, 0, MOE_BLOCK)
    it_hi = jnp.clip(pick(ends) - it_blk * MOE_BLOCK, 0, MOE_BLOCK)
    it_blk = jnp.where(live, it_blk, NBLK - 1)
    it_lo = jnp.where(live, it_lo, 0)
    it_hi = jnp.where(live, it_hi, 0)
    tok = jnp.arange(T, dtype=jnp.int32)[:, None]
    key_s, gate_s = lax.sort(((top_e * T + tok).reshape(A), gate.reshape(A)), num_keys=1)
    items = tuple(v.astype(jnp.int32) for v in (it_blk, it_e, it_lo, it_hi))
    group_rows = BLOCK_GROUP * MOE_BLOCK
    return (items, (key_s % T).reshape(A // group_rows, 1, group_rows),
            gate_s.reshape(A // group_rows, BLOCK_GROUP, MOE_BLOCK), starts)


def kernel(x, w_in, b_in, sinks, lam_q1, lam_k1, lam_q2, lam_k2, diff_norm_g, w_out, ln1_g, ln1_b,
           w_router, router_bias, w_gate, w_up, w_down, ws_gate, ws_up, ws_down, ln2_g, ln2_b):
    B, S, D = x.shape
    assert D == D_MODEL and w_in.shape[0] == DEPTH
    T = B * S
    assert B == 1 and T % 512 == 0
    cos128, sin128 = _rope_tables(S)
    x2 = x.reshape(T, D)
    for l in range(DEPTH):
        lam_init = 0.8 - 0.6 * math.exp(-0.3 * l)
        wi = w_in[l]
        wi = jnp.concatenate([_swa_head_order(wi[:, :_AK], 1), wi[:, _AK:]], axis=1).astype(BF16)
        bi = jnp.concatenate([_swa_head_order(b_in[l][:_AK], 0), b_in[l][_AK:]])[None, :]
        wo = jnp.concatenate([_swa_head_order(w_out[l][:_AK], 0), w_out[l][_AK:]], axis=0).astype(BF16)
        sink_p = sinks[l].astype(F32)
        lam = (jnp.exp(jnp.sum(lam_q1[l].astype(F32) * lam_k1[l].astype(F32)))
               - jnp.exp(jnp.sum(lam_q2[l].astype(F32) * lam_k2[l].astype(F32))) + lam_init).reshape(1)
        wr = w_router[l]
        wr_hi = wr.astype(BF16)
        wr_lo = (wr - wr_hi.astype(F32)).astype(BF16)

        qa, ka, va, qd, kd, vd = _in_proj(x2, wi, bi, cos128, sin128)
        oa = _swa(sink_p, qa, ka, va)
        ob = _diff_attn(lam, qd, kd, vd, diff_norm_g[l][None, :], lam_init)
        x1, x1t, logits = _out_proj(oa, ob, x2, wo, ln1_g[l][None, :], ln1_b[l][None, :], wr_hi, wr_lo)
        idx, gates, rank, counts = _route(logits, router_bias[l][None, :])
        items, tok_blocks, gate_blocks, starts = _item_tables(idx[:, :TOP_K], gates[:, :TOP_K], counts[0], T)
        pos = _row_pos(idx, rank, starts.astype(F32)[None, :])[:, :TOP_K]
        ys = _experts(items, tok_blocks, gate_blocks, x1t, w_gate[l], w_up[l], w_down[l])
        tt = 128
        pos_blocks = pos.reshape(T // tt, tt, TOP_K).transpose(0, 2, 1).reshape(T // tt, 1, TOP_K * tt)
        x2 = _combine(pos_blocks, ys, x1, ws_gate[l].astype(BF16), ws_up[l].astype(BF16),
                      ws_down[l].astype(BF16), ln2_g[l][None, :], ln2_b[l][None, :], tt=tt)
    return x2.reshape(B, S, D)
```

```python
import functools
import math

import numpy as np
import jax
import jax.numpy as jnp
from jax import lax
from jax.experimental import pallas as pl
from jax.experimental.pallas import tpu as pltpu

D_MODEL = 1024
CHUNK = 64
HEAD_DIM = 64
ROPE_THETA = 10000.0
SWA_Q_HEADS = 8
SWA_KV_HEADS = 2
SWA_WIN_CHUNKS = 2
DIFF_HEADS = 4
DIFF_V_DIM = 2 * HEAD_DIM
N_EXPERTS = 256
TOP_K = 8
N_GROUPS = 8
TOPK_GROUPS = 4
GROUP_SIZE = N_EXPERTS // N_GROUPS
EXPERT_FF = 256
SHARED_FF = 256
ROUTED_SCALE = 2.5
MOE_BLOCK = 128
DEPTH = 1
DEEPNORM_ALPHA = (2.0 * DEPTH) ** 0.25
NORM_EPS = 1e-5
LOG2E = math.log2(math.e)
NEG_BIG = -1e30
LANES = 128
SUBLANES = 8

_AQ, _AK, _AV, _BQ, _BK, _BV, _END = 0, 512, 640, 768, 1280, 1792, 2304
IN_WIDTH = _END

_NT = (((1,), (1,)), ((), ()))

F32 = jnp.float32
BF16 = jnp.bfloat16


def _cparams(vmem_mb, n_axes=1):
    return pltpu.CompilerParams(dimension_semantics=("arbitrary",) * n_axes,
                                vmem_limit_bytes=vmem_mb * 1024 * 1024)


def _in_proj_kernel(x_ref, w_ref, b_ref, cos_ref, sin_ref,
                    qa_ref, ka_ref, va_ref, qd_ref, kd_ref, vd_ref):
    tm = x_ref.shape[0]
    x = x_ref[...].astype(BF16)
    cos = cos_ref[...]
    sin = sin_ref[...]
    lane = lax.broadcasted_iota(jnp.int32, (tm, LANES), 1)
    first_half = (lane % HEAD_DIM) < (HEAD_DIM // 2)
    q_scale = HEAD_DIM ** -0.5 * LOG2E

    def proj(c0, c1):
        return jnp.dot(x, w_ref[:, c0:c1], preferred_element_type=F32) + b_ref[:, c0:c1]

    def rope(c):
        rot = jnp.where(first_half, pltpu.roll(c, LANES - 32, 1), pltpu.roll(c, 32, 1))
        return c * cos + rot * sin

    def emit(out_ref, c0, c1, rotary, scale):
        for g0 in range(c0, c1, 256):
            g1 = min(g0 + 256, c1)
            p = proj(g0, g1)
            for j in range(0, g1 - g0, LANES):
                c = p[:, j:j + LANES]
                if rotary:
                    c = rope(c)
                if scale != 1.0:
                    c = c * scale
                out_ref[:, g0 - c0 + j:g0 - c0 + j + LANES] = c.astype(out_ref.dtype)

    emit(qa_ref, _AQ, _AK, True, q_scale)
    emit(ka_ref, _AK, _AV, True, 1.0)
    emit(va_ref, _AV, _BQ, False, 1.0)
    emit(qd_ref, _BQ, _BK, True, q_scale)
    emit(kd_ref, _BK, _BV, True, 1.0)
    emit(vd_ref, _BV, _END, False, 1.0)


def _in_proj(x2, w_in, b_in, cos128, sin128, tm=512):
    T = x2.shape[0]
    outs = [(T, 512), (T, 128), (T, 128), (T, 512), (T, 512), (T, 512)]
    return pl.pallas_call(
        _in_proj_kernel,
        grid=(T // tm,),
        in_specs=[
            pl.BlockSpec((tm, D_MODEL), lambda i: (i, 0)),
            pl.BlockSpec((D_MODEL, IN_WIDTH), lambda i: (0, 0)),
            pl.BlockSpec((1, IN_WIDTH), lambda i: (0, 0)),
            pl.BlockSpec((tm, LANES), lambda i: (i, 0)),
            pl.BlockSpec((tm, LANES), lambda i: (i, 0)),
        ],
        out_specs=[pl.BlockSpec((tm, s[1]), lambda i: (i, 0)) for s in outs],
        out_shape=[jax.ShapeDtypeStruct(s, BF16) for s in outs],
        compiler_params=_cparams(48),
        name="in_proj_rope",
    )(x2, w_in, b_in, cos128, sin128)


def _swa_kernel(sink_ref, q_ref, kp_ref, kc_ref, vp_ref, vc_ref, o_ref):
    i = pl.program_id(0)
    tq = q_ref.shape[0]
    back = SWA_WIN_CHUNKS * CHUNK
    keys = jnp.concatenate([kp_ref[tq - back:, :], kc_ref[...]], axis=0)
    vals = jnp.concatenate([vp_ref[tq - back:, :], vc_ref[...]], axis=0)
    nk = tq + back
    qc = lax.broadcasted_iota(jnp.int32, (tq, nk), 0) // CHUNK
    kc = lax.broadcasted_iota(jnp.int32, (tq, nk), 1) // CHUNK - SWA_WIN_CHUNKS
    lo = jnp.where(i > 0, qc - SWA_WIN_CHUNKS, jnp.maximum(qc - SWA_WIN_CHUNKS, 0))
    bias = jnp.where(kc <= qc, jnp.where(kc >= lo, 0.0, NEG_BIG), NEG_BIG).astype(F32)
    lane = lax.broadcasted_iota(jnp.int32, (tq, LANES), 1)
    low = lane < HEAD_DIM
    zero = jnp.zeros((), BF16)
    for c in range(SWA_Q_HEADS // 2):
        qch = q_ref[:, c * LANES:(c + 1) * LANES]
        outs = []
        for half in range(2):
            h = c + (SWA_Q_HEADS // 2) * half
            qm = jnp.where(low if half == 0 else jnp.logical_not(low), qch, zero)
            s = lax.dot_general(qm, keys, _NT, preferred_element_type=F32) + bias
            sink = sink_ref[h] * LOG2E
            m = jnp.maximum(jnp.max(s, axis=-1, keepdims=True), sink)
            p = jnp.exp2(s - m)
            denom = jnp.sum(p, axis=-1, keepdims=True) + jnp.exp2(sink - m)
            o = jnp.dot(p.astype(BF16), vals, preferred_element_type=F32)
            outs.append(o / denom)
        o_ref[:, c * LANES:(c + 1) * LANES] = jnp.where(low, outs[0], outs[1]).astype(o_ref.dtype)


def _swa(sinks, qa, ka, va, tq=256):
    T = qa.shape[0]
    prev = lambda i: (jnp.maximum(i - 1, 0), 0)
    cur = lambda i: (i, 0)
    return pl.pallas_call(
        _swa_kernel,
        grid=(T // tq,),
        in_specs=[
            pl.BlockSpec(memory_space=pltpu.SMEM),
            pl.BlockSpec((tq, 512), cur),
            pl.BlockSpec((tq, LANES), prev),
            pl.BlockSpec((tq, LANES), cur),
            pl.BlockSpec((tq, LANES), prev),
            pl.BlockSpec((tq, LANES), cur),
        ],
        out_specs=pl.BlockSpec((tq, 512), cur),
        out_shape=jax.ShapeDtypeStruct((T, 512), BF16),
        compiler_params=_cparams(32),
        name="swa_sink_attention",
    )(sinks, qa, ka, ka, va, va)


KV_TILES_PER_TRIP = 4


def _diff_kernel(lam_ref, q_ref, k_ref, v_ref, g_ref, o_ref, acc_ref, *, lam_init, nh, tk):
    i = pl.program_id(1)
    tq = q_ref.shape[0]
    lane = lax.broadcasted_iota(jnp.int32, (tq, LANES), 1)
    zero = jnp.zeros((), BF16)
    qs = []
    for h in range(nh):
        q = q_ref[:, h * LANES:(h + 1) * LANES]
        qs.append(jnp.where(lane < HEAD_DIM, q, zero))
        qs.append(jnp.where(lane >= HEAD_DIM, q, zero))
    ones_col = jnp.where(lax.broadcasted_iota(jnp.int32, (tk, LANES), 1) == 0, 1.0, 0.0).astype(BF16)
    acc_ref[...] = jnp.zeros_like(acc_ref)

    def step(j, carry, diagonal):
        start = pl.multiple_of(j * tk, tk)
        new = []
        if diagonal:
            rc = (lax.broadcasted_iota(jnp.int32, (tq, tk), 0) + i * tq) // CHUNK
            cc = (lax.broadcasted_iota(jnp.int32, (tq, tk), 1) + j * tk) // CHUNK
            visible = cc <= rc
        for h in range(nh):
            k = k_ref[pl.ds(start, tk), h * LANES:(h + 1) * LANES]
            v = v_ref[pl.ds(start, tk), h * LANES:(h + 1) * LANES]
            vext = jnp.concatenate([v, ones_col], axis=1)
            for mi in range(2):
                c = 2 * h + mi
                s = lax.dot_general(qs[c], k, _NT, preferred_element_type=F32)
                if diagonal:
                    s = jnp.where(visible, s, NEG_BIG)
                m_old = carry[c]
                m_new = jnp.maximum(m_old, jnp.max(s, axis=-1, keepdims=True))
                alpha = jnp.exp2(m_old - m_new)
                p = jnp.exp2((s - m_new).astype(BF16))
                acc_ref[c] = acc_ref[c] * alpha + jnp.dot(p, vext, preferred_element_type=F32)
                new.append(m_new)
        return tuple(new)

    init = tuple(jnp.full((tq, 1), NEG_BIG, F32) for _ in range(2 * nh))
    nfull = i * (tq // tk)
    def trip(jj, c):
        for t in range(KV_TILES_PER_TRIP):
            c = step(KV_TILES_PER_TRIP * jj + t, c, False)
        return c

    ntrip = nfull // KV_TILES_PER_TRIP
    carry = lax.fori_loop(0, ntrip, trip, init)
    carry = lax.fori_loop(KV_TILES_PER_TRIP * ntrip, nfull, lambda j, c: step(j, c, False), carry)
    for d in range(tq // tk):
        carry = step(nfull + d, carry, True)
    lam = lam_ref[0]
    for h in range(nh):
        a0 = acc_ref[2 * h]
        a1 = acc_ref[2 * h + 1]
        o = (a0[:, :DIFF_V_DIM] / a0[:, DIFF_V_DIM:DIFF_V_DIM + 1]
             - lam * (a1[:, :DIFF_V_DIM] / a1[:, DIFF_V_DIM:DIFF_V_DIM + 1]))
        ms = jnp.mean(o * o, axis=-1, keepdims=True)
        o = o * lax.rsqrt(ms + NORM_EPS) * g_ref[...] * (1.0 - lam_init)
        o_ref[:, h * LANES:(h + 1) * LANES] = o.astype(o_ref.dtype)


def _diff_attn(lam, qd, kd, vd, sub_g, lam_init, tq=512, tk=512, nh=DIFF_HEADS):
    T = qd.shape[0]
    W = nh * LANES
    whole = lambda h, i: (0, h)
    return pl.pallas_call(
        functools.partial(_diff_kernel, lam_init=lam_init, nh=nh, tk=tk),
        grid=(DIFF_HEADS // nh, T // tq),
        in_specs=[
            pl.BlockSpec(memory_space=pltpu.SMEM),
            pl.BlockSpec((tq, W), lambda h, i: (i, h)),
            pl.BlockSpec((T, W), whole, pipeline_mode=pl.Buffered(1)),
            pl.BlockSpec((T, W), whole, pipeline_mode=pl.Buffered(1)),
            pl.BlockSpec((1, LANES), lambda h, i: (0, 0)),
        ],
        out_specs=pl.BlockSpec((tq, W), lambda h, i: (i, h)),
        out_shape=jax.ShapeDtypeStruct((T, DIFF_HEADS * DIFF_V_DIM), BF16),
        scratch_shapes=[pltpu.VMEM((2 * nh, tq, 2 * LANES), F32)],
        compiler_params=_cparams(56, 2),
        name="diff_attention",
    )(lam, qd, kd, vd, sub_g)


def _from_token_tiles(v):
    v = jnp.swapaxes(v, 0, 1)
    return jnp.concatenate([v[s] for s in range(SUBLANES)], axis=1)


def _to_token_tiles(v):
    return jnp.swapaxes(jnp.stack([v[:, s * LANES:(s + 1) * LANES] for s in range(SUBLANES)], axis=0), 0, 1)


def _layer_norm(y, g, b):
    mu = jnp.mean(y, axis=-1, keepdims=True)
    d = y - mu
    var = jnp.mean(d * d, axis=-1, keepdims=True)
    return d * lax.rsqrt(var + NORM_EPS) * g + b


def _out_proj_kernel(oa_ref, ob_ref, x_ref, wo_ref, g_ref, b_ref, wrh_ref, wrl_ref, x1_ref, x1t_ref, logit_ref):
    half = oa_ref.shape[1]
    mix = jnp.dot(oa_ref[...], wo_ref[:half, :], preferred_element_type=F32)
    mix = mix + jnp.dot(ob_ref[...], wo_ref[half:, :], preferred_element_type=F32)
    x1 = _layer_norm(DEEPNORM_ALPHA * x_ref[...] + mix, g_ref[...], b_ref[...])
    x1_ref[...] = x1
    x1t_ref[...] = _to_token_tiles(x1)
    xh = x1.astype(BF16)
    xl = (x1 - xh.astype(F32)).astype(BF16)
    wh = wrh_ref[...]
    logits = jnp.dot(xh, wh, preferred_element_type=F32)
    logits = logits + jnp.dot(xl, wh, preferred_element_type=F32)
    logits = logits + jnp.dot(xh, wrl_ref[...], preferred_element_type=F32)
    logit_ref[...] = logits


def _out_proj(oa, ob, x2, w_out, g, b, wr_hi, wr_lo, tm=512):
    T = x2.shape[0]
    row = lambda i: (i, 0)
    fix = lambda i: (0, 0)
    return pl.pallas_call(
        _out_proj_kernel,
        grid=(T // tm,),
        in_specs=[
            pl.BlockSpec((tm, 512), row),
            pl.BlockSpec((tm, 512), row),
            pl.BlockSpec((tm, D_MODEL), row),
            pl.BlockSpec((D_MODEL, D_MODEL), fix),
            pl.BlockSpec((1, D_MODEL), fix),
            pl.BlockSpec((1, D_MODEL), fix),
            pl.BlockSpec((D_MODEL, N_EXPERTS), fix),
            pl.BlockSpec((D_MODEL, N_EXPERTS), fix),
        ],
        out_specs=[pl.BlockSpec((tm, D_MODEL), row), pl.BlockSpec((tm, SUBLANES, LANES), lambda i: (i, 0, 0)),
                   pl.BlockSpec((tm, N_EXPERTS), row)],
        out_shape=[jax.ShapeDtypeStruct((T, D_MODEL), F32), jax.ShapeDtypeStruct((T, SUBLANES, LANES), F32),
                   jax.ShapeDtypeStruct((T, N_EXPERTS), F32)],
        compiler_params=_cparams(48),
        name="out_proj_ln1_router",
    )(oa, ob, x2, w_out, g, b, wr_hi, wr_lo)


def _route_kernel(logit_ref, bias_ref, idx_ref, gate_ref, rank_ref, count_ref, run_ref):
    i = pl.program_id(0)
    tm = logit_ref.shape[0]
    scores = 1.0 / (1.0 + jnp.exp(-logit_ref[...]))
    sel = scores + bias_ref[...]
    lane = lax.broadcasted_iota(jnp.int32, (tm, N_EXPERTS), 1)
    lanef = lane.astype(F32)
    grp = lane // GROUP_SIZE
    neg_inf = -jnp.inf

    @pl.when(i == 0)
    def _():
        run_ref[...] = jnp.zeros_like(run_ref)

    def first_argmax(v):
        m = jnp.max(v, axis=-1, keepdims=True)
        idx = jnp.min(jnp.where(v == m, lanef, float(N_EXPERTS)), axis=-1, keepdims=True)
        return m, idx

    gscore = []
    for g in range(N_GROUPS):
        vg = jnp.where(grp == g, sel, neg_inf)
        m1, i1 = first_argmax(vg)
        m2 = jnp.max(jnp.where(lanef == i1, neg_inf, vg), axis=-1, keepdims=True)
        gscore.append(m1 + m2)
    keepf = jnp.zeros((tm, N_EXPERTS), F32)
    for g in range(N_GROUPS):
        ahead_n = jnp.zeros((tm, 1), F32)
        for o in range(N_GROUPS):
            if o == g:
                continue
            ahead = (gscore[o] > gscore[g]) if o > g else (gscore[o] >= gscore[g])
            ahead_n = ahead_n + jnp.where(ahead, 1.0, 0.0)
        keepf = jnp.where(grp == g, jnp.where(ahead_n < float(TOPK_GROUPS), 1.0, 0.0), keepf)
    cand = jnp.where(keepf > 0.5, sel, neg_inf)

    out_lane = lax.broadcasted_iota(jnp.int32, (tm, LANES), 1)
    idx_acc = jnp.zeros((tm, LANES), F32)
    sc_acc = jnp.zeros((tm, LANES), F32)
    hits = []
    chosen = jnp.zeros((tm, N_EXPERTS), F32)
    for k in range(TOP_K):
        _, ik = first_argmax(cand)
        hit = lanef == ik
        hits.append(hit)
        chosen = jnp.where(hit, 1.0, chosen)
        sk = jnp.sum(jnp.where(hit, scores, 0.0), axis=-1, keepdims=True)
        idx_acc = jnp.where(out_lane == k, ik, idx_acc)
        sc_acc = jnp.where(out_lane == k, sk, sc_acc)
        cand = jnp.where(hit, neg_inf, cand)
    total = jnp.sum(sc_acc, axis=-1, keepdims=True)
    idx_ref[...] = idx_acc.astype(jnp.int32)
    gate_ref[...] = sc_acc / total * ROUTED_SCALE

    r = lax.broadcasted_iota(jnp.int32, (tm, tm), 0)
    c = lax.broadcasted_iota(jnp.int32, (tm, tm), 1)
    lower = jnp.where(c < r, 1.0, 0.0).astype(BF16)
    before = jnp.dot(lower, chosen.astype(BF16), preferred_element_type=F32) + run_ref[...]
    rank_acc = jnp.zeros((tm, LANES), F32)
    for k in range(TOP_K):
        rk = jnp.sum(jnp.where(hits[k], before, 0.0), axis=-1, keepdims=True)
        rank_acc = jnp.where(out_lane == k, rk, rank_acc)
    rank_ref[...] = rank_acc.astype(jnp.int32)
    run_ref[...] = run_ref[...] + jnp.sum(chosen, axis=0, keepdims=True)
    count_ref[...] = run_ref[...].astype(jnp.int32)


def _route(logits, r_bias, tm=512):
    T = logits.shape[0]
    row = lambda i: (i, 0)
    fix = lambda i: (0, 0)
    return pl.pallas_call(
        _route_kernel,
        grid=(T // tm,),
        in_specs=[pl.BlockSpec((tm, N_EXPERTS), row), pl.BlockSpec((1, N_EXPERTS), fix)],
        out_specs=[pl.BlockSpec((tm, LANES), row), pl.BlockSpec((tm, LANES), row), pl.BlockSpec((tm, LANES), row),
                   pl.BlockSpec((1, N_EXPERTS), fix)],
        out_shape=[jax.ShapeDtypeStruct((T, LANES), jnp.int32), jax.ShapeDtypeStruct((T, LANES), F32),
                   jax.ShapeDtypeStruct((T, LANES), jnp.int32), jax.ShapeDtypeStruct((1, N_EXPERTS), jnp.int32)],
        scratch_shapes=[pltpu.VMEM((1, N_EXPERTS), F32)],
        compiler_params=_cparams(32),
        name="moe_route",
    )(logits, r_bias)


def _row_pos_kernel(idx_ref, rank_ref, pstart_ref, pos_ref):
    tm = idx_ref.shape[0]
    lanef = lax.broadcasted_iota(jnp.int32, (tm, N_EXPERTS), 1).astype(F32)
    out_lane = lax.broadcasted_iota(jnp.int32, (tm, LANES), 1)
    idx = idx_ref[...].astype(F32)
    pstart = pstart_ref[...]
    base = jnp.zeros((tm, LANES), F32)
    for k in range(TOP_K):
        ik = idx[:, k:k + 1]
        pk = jnp.sum(jnp.where(lanef == ik, pstart, 0.0), axis=-1, keepdims=True)
        base = jnp.where(out_lane == k, pk, base)
    pos_ref[...] = base.astype(jnp.int32) + rank_ref[...]


def _row_pos(idx, rank, pstarts, tm=512):
    T = idx.shape[0]
    row = lambda i: (i, 0)
    return pl.pallas_call(
        _row_pos_kernel,
        grid=(T // tm,),
        in_specs=[pl.BlockSpec((tm, LANES), row), pl.BlockSpec((tm, LANES), row),
                  pl.BlockSpec((1, N_EXPERTS), lambda i: (0, 0))],
        out_specs=pl.BlockSpec((tm, LANES), row),
        out_shape=jax.ShapeDtypeStruct((T, LANES), jnp.int32),
        compiler_params=_cparams(32),
        name="moe_row_pos",
    )(idx, rank, pstarts)


GATHER_UNROLL = 8


def _start_row_gather(idx_ref, n_rows, src_hbm, dst, row0, sem, unrolled=False, idx0=0):
    ids = idx_ref.at[0, 0, pl.ds(idx0, n_rows)]

    def start(r, priority):
        t = ids[r]
        pltpu.make_async_copy(src_hbm.at[t], dst.at[row0 + r], sem).start(priority=priority)

    if unrolled:
        for r in range(n_rows):
            start(r, r % 2)
    else:
        def body(g, carry):
            for j in range(GATHER_UNROLL):
                start(g * GATHER_UNROLL + j, j % 2)
            return carry
        lax.fori_loop(0, n_rows // GATHER_UNROLL, body, 0)


def _wait_row_gather(n_rows, src_hbm, dst, row0, sem):
    pltpu.make_async_copy(src_hbm.at[pl.ds(0, n_rows)], dst.at[pl.ds(row0, n_rows)], sem).wait()


GATHER_SLOTS = 3
BLOCK_GROUP = 8


def _expert_kernel(blk_ref, e_ref, lo_ref, hi_ref, tok0_ref, tok1_ref, tok2_ref, gate_ref, x_hbm,
                   wg_ref, wu_ref, wd_ref, y_ref, buf, sem):
    del e_ref
    i = pl.program_id(0)
    n = pl.num_programs(0)
    rows = MOE_BLOCK

    def in_group(d):
        return (blk_ref[jnp.minimum(i + d, n - 1)] % BLOCK_GROUP) * rows

    @pl.when(i == 0)
    def _():
        _start_row_gather(tok0_ref, rows, x_hbm, buf, 0, sem.at[0], idx0=in_group(0))
        _start_row_gather(tok1_ref, rows, x_hbm, buf, rows, sem.at[1], idx0=in_group(1))

    @pl.when(jnp.logical_or(i == 0, blk_ref[i] != blk_ref[jnp.maximum(i - 1, 0)]))
    def _():
        y_ref[...] = jnp.zeros_like(y_ref)

    slot = i % GATHER_SLOTS
    row0 = pl.multiple_of(slot * rows, rows)
    _wait_row_gather(rows, x_hbm, buf, row0, sem.at[slot])
    xg = _from_token_tiles(buf[pl.ds(row0, rows)]).astype(BF16)
    nslot = (i + 2) % GATHER_SLOTS
    _start_row_gather(tok2_ref, rows, x_hbm, buf, nslot * rows, sem.at[nslot], unrolled=True, idx0=in_group(2))
    a = jnp.dot(xg, wg_ref[...].astype(BF16), preferred_element_type=F32)
    u = jnp.dot(xg, wu_ref[...].astype(BF16), preferred_element_type=F32)
    h = (a / (1.0 + jnp.exp(-a)) * u).astype(BF16)
    y = jnp.dot(h, wd_ref[...].astype(BF16), preferred_element_type=F32)
    r = lax.broadcasted_iota(jnp.int32, (rows, 1), 0)
    mine = jnp.logical_and(r >= lo_ref[i], r < hi_ref[i])
    grow = gate_ref[0, pl.ds(blk_ref[i] % BLOCK_GROUP, 1), :]
    eye = lax.broadcasted_iota(jnp.int32, (rows, rows), 0) == lax.broadcasted_iota(jnp.int32, (rows, rows), 1)
    gate = jnp.sum(jnp.where(eye, grow, 0.0), axis=1, keepdims=True)
    y_ref[...] += _to_token_tiles(jnp.where(mine, y * gate, 0.0))

    @pl.when(i == n - 1)
    def _():
        for d in (1, 2):
            s = (i + d) % GATHER_SLOTS
            _wait_row_gather(rows, x_hbm, buf, s * rows, sem.at[s])


def _experts(items, tok_blocks, gate_blocks, x1, w_gate, w_up, w_down):
    it_blk, it_e, it_lo, it_hi = items
    NI = it_blk.shape[0]
    NBLK = tok_blocks.shape[0] * BLOCK_GROUP
    ahead = lambda d: (lambda i, blk, e, lo, hi: (blk[jnp.minimum(i + d, NI - 1)] // BLOCK_GROUP, 0, 0))
    wsel = lambda i, blk, e, lo, hi: (e[i], 0, 0)
    group_rows = BLOCK_GROUP * MOE_BLOCK
    grid_spec = pltpu.PrefetchScalarGridSpec(
        num_scalar_prefetch=4,
        grid=(NI,),
        in_specs=[
            pl.BlockSpec((1, 1, group_rows), ahead(0), memory_space=pltpu.SMEM),
            pl.BlockSpec((1, 1, group_rows), ahead(1), memory_space=pltpu.SMEM),
            pl.BlockSpec((1, 1, group_rows), ahead(2), memory_space=pltpu.SMEM),
            pl.BlockSpec((1, BLOCK_GROUP, MOE_BLOCK), ahead(0)),
            pl.BlockSpec(memory_space=pl.ANY),
            pl.BlockSpec((None, D_MODEL, EXPERT_FF), wsel),
            pl.BlockSpec((None, D_MODEL, EXPERT_FF), wsel),
            pl.BlockSpec((None, EXPERT_FF, D_MODEL), wsel),
        ],
        out_specs=pl.BlockSpec((MOE_BLOCK, SUBLANES, LANES), lambda i, blk, e, lo, hi: (blk[i], 0, 0)),
        scratch_shapes=[pltpu.VMEM((GATHER_SLOTS * MOE_BLOCK, SUBLANES, LANES), F32),
                        pltpu.SemaphoreType.DMA((GATHER_SLOTS,))],
    )
    return pl.pallas_call(
        _expert_kernel,
        grid_spec=grid_spec,
        out_shape=jax.ShapeDtypeStruct((NBLK * MOE_BLOCK, SUBLANES, LANES), F32),
        compiler_params=_cparams(48),
        name="moe_experts",
    )(it_blk, it_e, it_lo, it_hi, tok_blocks, tok_blocks, tok_blocks, gate_blocks, x1, w_gate, w_up, w_down)


def _combine_kernel(posc_ref, posn_ref, ys_hbm, x1_ref, wsg_ref, wsu_ref, wsd_ref, g_ref, b_ref, out_ref, buf, sem):
    i = pl.program_id(0)
    n = pl.num_programs(0)
    tt = x1_ref.shape[0]
    rows = TOP_K * tt
    slot = i % 2

    @pl.when(i == 0)
    def _():
        _start_row_gather(posc_ref, rows, ys_hbm, buf, 0, sem.at[0])

    @pl.when(i + 1 < n)
    def _():
        nslot = (i + 1) % 2
        _start_row_gather(posn_ref, rows, ys_hbm, buf, nslot * rows, sem.at[nslot])

    x1 = x1_ref[...]
    xb = x1.astype(BF16)
    a = jnp.dot(xb, wsg_ref[...], preferred_element_type=F32)
    u = jnp.dot(xb, wsu_ref[...], preferred_element_type=F32)
    h = (a / (1.0 + jnp.exp(-a)) * u).astype(BF16)
    ffn = jnp.dot(h, wsd_ref[...], preferred_element_type=F32)

    row0 = pl.multiple_of(slot * rows, rows)
    _wait_row_gather(rows, ys_hbm, buf, row0, sem.at[slot])
    routed = buf[pl.ds(row0, tt)]
    for k in range(1, TOP_K):
        routed = routed + buf[pl.ds(row0 + k * tt, tt)]
    ffn = ffn + _from_token_tiles(routed)
    out_ref[...] = _layer_norm(DEEPNORM_ALPHA * x1 + ffn, g_ref[...], b_ref[...])


def _combine(pos_blocks, ys, x1, ws_gate, ws_up, ws_down, g, b, tt=128):
    T = x1.shape[0]
    n = T // tt
    rows = TOP_K * tt
    row = lambda i: (i, 0)
    fix = lambda i: (0, 0)
    return pl.pallas_call(
        _combine_kernel,
        grid=(n,),
        in_specs=[
            pl.BlockSpec((1, 1, rows), lambda i: (i, 0, 0), memory_space=pltpu.SMEM),
            pl.BlockSpec((1, 1, rows), lambda i: (jnp.minimum(i + 1, n - 1), 0, 0), memory_space=pltpu.SMEM),
            pl.BlockSpec(memory_space=pl.ANY),
            pl.BlockSpec((tt, D_MODEL), row),
            pl.BlockSpec((D_MODEL, SHARED_FF), fix),
            pl.BlockSpec((D_MODEL, SHARED_FF), fix),
            pl.BlockSpec((SHARED_FF, D_MODEL), fix),
            pl.BlockSpec((1, D_MODEL), fix),
            pl.BlockSpec((1, D_MODEL), fix),
        ],
        out_specs=pl.BlockSpec((tt, D_MODEL), row),
        out_shape=jax.ShapeDtypeStruct((T, D_MODEL), F32),
        scratch_shapes=[pltpu.VMEM((2 * rows, SUBLANES, LANES), F32), pltpu.SemaphoreType.DMA((2,))],
        compiler_params=_cparams(48),
        name="moe_combine_ln2",
    )(pos_blocks, pos_blocks, ys, x1, ws_gate, ws_up, ws_down, g, b)


def _rope_tables(T):
    inv = 1.0 / (ROPE_THETA ** (jnp.arange(0, HEAD_DIM, 2, dtype=F32) / HEAD_DIM))
    ang = jnp.arange(T, dtype=F32)[:, None] * inv[None, :]
    cos, sin = jnp.cos(ang), jnp.sin(ang)
    cos128 = jnp.tile(cos, (1, 4))
    sin128 = jnp.tile(jnp.concatenate([-sin, sin], axis=1), (1, 2))
    return cos128, sin128


def _swa_head_order(a, axis):
    half = SWA_Q_HEADS // 2
    shp = a.shape
    a = a.reshape(shp[:axis] + (2, half, HEAD_DIM) + shp[axis + 1:])
    return jnp.swapaxes(a, axis, axis + 1).reshape(shp)


def _item_tables(top_e, gate, counts, T):
    A = T * TOP_K
    NBLK = A // MOE_BLOCK
    NI = NBLK + N_EXPERTS - 1
    starts = jnp.cumsum(counts) - counts
    ends = starts + counts
    first_blk = starts // MOE_BLOCK
    nblk = jnp.where(counts > 0, (ends - 1) // MOE_BLOCK - first_blk + 1, 0)
    item_end = jnp.cumsum(nblk)
    item_off = item_end - nblk
    it = jnp.arange(NI, dtype=jnp.int32)
    live = it < item_end[-1]
    it_e = jnp.minimum(jnp.sum((item_end[None, :] <= it[:, None]).astype(jnp.int32), axis=1), N_EXPERTS - 1)
    onehot = (it_e[:, None] == jnp.arange(N_EXPERTS, dtype=jnp.int32)[None, :]).astype(jnp.int32)
    pick = lambda v: jnp.sum(onehot * v[None, :].astype(jnp.int32), axis=1)
    it_blk = pick(first_blk) + it - pick(item_off)
    it_lo = jnp.clip(pick(starts) - it_blk * MOE_BLOCK, 0, MOE_BLOCK)
    it_hi = jnp.clip(pick(ends) - it_blk * MOE_BLOCK, 0, MOE_BLOCK)
    it_blk = jnp.where(live, it_blk, NBLK - 1)
    it_lo = jnp.where(live, it_lo, 0)
    it_hi = jnp.where(live, it_hi, 0)
    tok = jnp.arange(T, dtype=jnp.int32)[:, None]
    key_s, gate_s = lax.sort(((top_e * T + tok).reshape(A), gate.reshape(A)), num_keys=1)
    items = tuple(v.astype(jnp.int32) for v in (it_blk, it_e, it_lo, it_hi))
    group_rows = BLOCK_GROUP * MOE_BLOCK
    return (items, (key_s % T).reshape(A // group_rows, 1, group_rows),
            gate_s.reshape(A // group_rows, BLOCK_GROUP, MOE_BLOCK), starts)


def kernel(x, w_in, b_in, sinks, lam_q1, lam_k1, lam_q2, lam_k2, diff_norm_g, w_out, ln1_g, ln1_b,
           w_router, router_bias, w_gate, w_up, w_down, ws_gate, ws_up, ws_down, ln2_g, ln2_b):
    B, S, D = x.shape
    assert D == D_MODEL and w_in.shape[0] == DEPTH
    T = B * S
    assert B == 1 and T % 512 == 0
    cos128, sin128 = _rope_tables(S)
    x2 = x.reshape(T, D)
    for l in range(DEPTH):
        lam_init = 0.8 - 0.6 * math.exp(-0.3 * l)
        wi = w_in[l]
        wi = jnp.concatenate([_swa_head_order(wi[:, :_AK], 1), wi[:, _AK:]], axis=1).astype(BF16)
        bi = jnp.concatenate([_swa_head_order(b_in[l][:_AK], 0), b_in[l][_AK:]])[None, :]
        wo = jnp.concatenate([_swa_head_order(w_out[l][:_AK], 0), w_out[l][_AK:]], axis=0).astype(BF16)
        sink_p = sinks[l].astype(F32)
        lam = (jnp.exp(jnp.sum(lam_q1[l].astype(F32) * lam_k1[l].astype(F32)))
               - jnp.exp(jnp.sum(lam_q2[l].astype(F32) * lam_k2[l].astype(F32))) + lam_init).reshape(1)
        wr = w_router[l]
        wr_hi = wr.astype(BF16)
        wr_lo = (wr - wr_hi.astype(F32)).astype(BF16)

        qa, ka, va, qd, kd, vd = _in_proj(x2, wi, bi, cos128, sin128)
        oa = _swa(sink_p, qa, ka, va)
        ob = _diff_attn(lam, qd, kd, vd, diff_norm_g[l][None, :], lam_init)
        x1, x1t, logits = _out_proj(oa, ob, x2, wo, ln1_g[l][None, :], ln1_b[l][None, :], wr_hi, wr_lo)
        idx, gates, rank, counts = _route(logits, router_bias[l][None, :])
        items, tok_blocks, gate_blocks, starts = _item_tables(idx[:, :TOP_K], gates[:, :TOP_K], counts[0], T)
        pos = _row_pos(idx, rank, starts.astype(F32)[None, :])[:, :TOP_K]
        ys = _experts(items, tok_blocks, gate_blocks, x1t, w_gate[l], w_up[l], w_down[l])
        tt = 256
        pos_blocks = pos.reshape(T // tt, tt, TOP_K).transpose(0, 2, 1).reshape(T // tt, 1, TOP_K * tt)
        x2 = _combine(pos_blocks, ys, x1, ws_gate[l].astype(BF16), ws_up[l].astype(BF16),
                      ws_down[l].astype(BF16), ln2_g[l][None, :], ln2_b[l][None, :], tt=tt)
    return x2.reshape(B, S, D)
```

```python
import functools
import math

import numpy as np
import jax
import jax.numpy as jnp
from jax import lax
from jax.experimental import pallas as pl
from jax.experimental.pallas import tpu as pltpu

D_MODEL = 1024
CHUNK = 64
HEAD_DIM = 64
ROPE_THETA = 10000.0
SWA_Q_HEADS = 8
SWA_KV_HEADS = 2
SWA_WIN_CHUNKS = 2
DIFF_HEADS = 4
DIFF_V_DIM = 2 * HEAD_DIM
N_EXPERTS = 256
TOP_K = 8
N_GROUPS = 8
TOPK_GROUPS = 4
GROUP_SIZE = N_EXPERTS // N_GROUPS
EXPERT_FF = 256
SHARED_FF = 256
ROUTED_SCALE = 2.5
MOE_BLOCK = 128
DEPTH = 1
DEEPNORM_ALPHA = (2.0 * DEPTH) ** 0.25
NORM_EPS = 1e-5
LOG2E = math.log2(math.e)
NEG_BIG = -1e30
LANES = 128
SUBLANES = 8

_AQ, _AK, _AV, _BQ, _BK, _BV, _END = 0, 512, 640, 768, 1280, 1792, 2304
IN_WIDTH = _END

_NT = (((1,), (1,)), ((), ()))

F32 = jnp.float32
BF16 = jnp.bfloat16


def _cparams(vmem_mb, n_axes=1):
    return pltpu.CompilerParams(dimension_semantics=("arbitrary",) * n_axes,
                                vmem_limit_bytes=vmem_mb * 1024 * 1024)


def _in_proj_kernel(x_ref, w_ref, b_ref, cos_ref, sin_ref,
                    qa_ref, ka_ref, va_ref, qd_ref, kd_ref, vd_ref):
    tm = x_ref.shape[0]
    x = x_ref[...].astype(BF16)
    cos = cos_ref[...]
    sin = sin_ref[...]
    lane = lax.broadcasted_iota(jnp.int32, (tm, LANES), 1)
    first_half = (lane % HEAD_DIM) < (HEAD_DIM // 2)
    q_scale = HEAD_DIM ** -0.5 * LOG2E

    def proj(c0, c1):
        return jnp.dot(x, w_ref[:, c0:c1], preferred_element_type=F32) + b_ref[:, c0:c1]

    def rope(c):
        rot = jnp.where(first_half, pltpu.roll(c, LANES - 32, 1), pltpu.roll(c, 32, 1))
        return c * cos + rot * sin

    def emit(out_ref, c0, c1, rotary, scale):
        for g0 in range(c0, c1, 256):
            g1 = min(g0 + 256, c1)
            p = proj(g0, g1)
            for j in range(0, g1 - g0, LANES):
                c = p[:, j:j + LANES]
                if rotary:
                    c = rope(c)
                if scale != 1.0:
                    c = c * scale
                out_ref[:, g0 - c0 + j:g0 - c0 + j + LANES] = c.astype(out_ref.dtype)

    emit(qa_ref, _AQ, _AK, True, q_scale)
    emit(ka_ref, _AK, _AV, True, 1.0)
    emit(va_ref, _AV, _BQ, False, 1.0)
    emit(qd_ref, _BQ, _BK, True, q_scale)
    emit(kd_ref, _BK, _BV, True, 1.0)
    emit(vd_ref, _BV, _END, False, 1.0)


def _in_proj(x2, w_in, b_in, cos128, sin128, tm=512):
    T = x2.shape[0]
    outs = [(T, 512), (T, 128), (T, 128), (T, 512), (T, 512), (T, 512)]
    return pl.pallas_call(
        _in_proj_kernel,
        grid=(T // tm,),
        in_specs=[
            pl.BlockSpec((tm, D_MODEL), lambda i: (i, 0)),
            pl.BlockSpec((D_MODEL, IN_WIDTH), lambda i: (0, 0)),
            pl.BlockSpec((1, IN_WIDTH), lambda i: (0, 0)),
            pl.BlockSpec((tm, LANES), lambda i: (i, 0)),
            pl.BlockSpec((tm, LANES), lambda i: (i, 0)),
        ],
        out_specs=[pl.BlockSpec((tm, s[1]), lambda i: (i, 0)) for s in outs],
        out_shape=[jax.ShapeDtypeStruct(s, BF16) for s in outs],
        compiler_params=_cparams(48),
        name="in_proj_rope",
    )(x2, w_in, b_in, cos128, sin128)


def _swa_kernel(sink_ref, q_ref, kp_ref, kc_ref, vp_ref, vc_ref, o_ref):
    i = pl.program_id(0)
    tq = q_ref.shape[0]
    back = SWA_WIN_CHUNKS * CHUNK
    keys = jnp.concatenate([kp_ref[tq - back:, :], kc_ref[...]], axis=0)
    vals = jnp.concatenate([vp_ref[tq - back:, :], vc_ref[...]], axis=0)
    nk = tq + back
    qc = lax.broadcasted_iota(jnp.int32, (tq, nk), 0) // CHUNK
    kc = lax.broadcasted_iota(jnp.int32, (tq, nk), 1) // CHUNK - SWA_WIN_CHUNKS
    lo = jnp.where(i > 0, qc - SWA_WIN_CHUNKS, jnp.maximum(qc - SWA_WIN_CHUNKS, 0))
    bias = jnp.where(kc <= qc, jnp.where(kc >= lo, 0.0, NEG_BIG), NEG_BIG).astype(F32)
    lane = lax.broadcasted_iota(jnp.int32, (tq, LANES), 1)
    low = lane < HEAD_DIM
    zero = jnp.zeros((), BF16)
    for c in range(SWA_Q_HEADS // 2):
        qch = q_ref[:, c * LANES:(c + 1) * LANES]
        outs = []
        for half in range(2):
            h = c + (SWA_Q_HEADS // 2) * half
            qm = jnp.where(low if half == 0 else jnp.logical_not(low), qch, zero)
            s = lax.dot_general(qm, keys, _NT, preferred_element_type=F32) + bias
            sink = sink_ref[h] * LOG2E
            m = jnp.maximum(jnp.max(s, axis=-1, keepdims=True), sink)
            p = jnp.exp2(s - m)
            denom = jnp.sum(p, axis=-1, keepdims=True) + jnp.exp2(sink - m)
            o = jnp.dot(p.astype(BF16), vals, preferred_element_type=F32)
            outs.append(o / denom)
        o_ref[:, c * LANES:(c + 1) * LANES] = jnp.where(low, outs[0], outs[1]).astype(o_ref.dtype)


def _swa(sinks, qa, ka, va, tq=256):
    T = qa.shape[0]
    prev = lambda i: (jnp.maximum(i - 1, 0), 0)
    cur = lambda i: (i, 0)
    return pl.pallas_call(
        _swa_kernel,
        grid=(T // tq,),
        in_specs=[
            pl.BlockSpec(memory_space=pltpu.SMEM),
            pl.BlockSpec((tq, 512), cur),
            pl.BlockSpec((tq, LANES), prev),
            pl.BlockSpec((tq, LANES), cur),
            pl.BlockSpec((tq, LANES), prev),
            pl.BlockSpec((tq, LANES), cur),
        ],
        out_specs=pl.BlockSpec((tq, 512), cur),
        out_shape=jax.ShapeDtypeStruct((T, 512), BF16),
        compiler_params=_cparams(32),
        name="swa_sink_attention",
    )(sinks, qa, ka, ka, va, va)


KV_TILES_PER_TRIP = 4


def _diff_kernel(lam_ref, q_ref, k_ref, v_ref, g_ref, o_ref, acc_ref, *, lam_init, nh, tk):
    i = pl.program_id(1)
    tq = q_ref.shape[0]
    lane = lax.broadcasted_iota(jnp.int32, (tq, LANES), 1)
    zero = jnp.zeros((), BF16)
    qs = []
    for h in range(nh):
        q = q_ref[:, h * LANES:(h + 1) * LANES]
        qs.append(jnp.where(lane < HEAD_DIM, q, zero))
        qs.append(jnp.where(lane >= HEAD_DIM, q, zero))
    ones_col = jnp.where(lax.broadcasted_iota(jnp.int32, (tk, LANES), 1) == 0, 1.0, 0.0).astype(BF16)
    acc_ref[...] = jnp.zeros_like(acc_ref)

    def step(j, carry, diagonal):
        start = pl.multiple_of(j * tk, tk)
        new = []
        if diagonal:
            rc = (lax.broadcasted_iota(jnp.int32, (tq, tk), 0) + i * tq) // CHUNK
            cc = (lax.broadcasted_iota(jnp.int32, (tq, tk), 1) + j * tk) // CHUNK
            visible = cc <= rc
        for h in range(nh):
            k = k_ref[pl.ds(start, tk), h * LANES:(h + 1) * LANES]
            v = v_ref[pl.ds(start, tk), h * LANES:(h + 1) * LANES]
            vext = jnp.concatenate([v, ones_col], axis=1)
            for mi in range(2):
                c = 2 * h + mi
                s = lax.dot_general(qs[c], k, _NT, preferred_element_type=F32)
                if diagonal:
                    s = jnp.where(visible, s, NEG_BIG)
                m_old = carry[c]
                m_new = jnp.maximum(m_old, jnp.max(s, axis=-1, keepdims=True))
                alpha = jnp.exp2(m_old - m_new)
                p = jnp.exp2((s - m_new).astype(BF16))
                acc_ref[c] = acc_ref[c] * alpha + jnp.dot(p, vext, preferred_element_type=F32)
                new.append(m_new)
        return tuple(new)

    init = tuple(jnp.full((tq, 1), NEG_BIG, F32) for _ in range(2 * nh))
    nfull = i * (tq // tk)
    def trip(jj, c):
        for t in range(KV_TILES_PER_TRIP):
            c = step(KV_TILES_PER_TRIP * jj + t, c, False)
        return c

    ntrip = nfull // KV_TILES_PER_TRIP
    carry = lax.fori_loop(0, ntrip, trip, init)
    carry = lax.fori_loop(KV_TILES_PER_TRIP * ntrip, nfull, lambda j, c: step(j, c, False), carry)
    for d in range(tq // tk):
        carry = step(nfull + d, carry, True)
    lam = lam_ref[0]
    for h in range(nh):
        a0 = acc_ref[2 * h]
        a1 = acc_ref[2 * h + 1]
        o = (a0[:, :DIFF_V_DIM] / a0[:, DIFF_V_DIM:DIFF_V_DIM + 1]
             - lam * (a1[:, :DIFF_V_DIM] / a1[:, DIFF_V_DIM:DIFF_V_DIM + 1]))
        ms = jnp.mean(o * o, axis=-1, keepdims=True)
        o = o * lax.rsqrt(ms + NORM_EPS) * g_ref[...] * (1.0 - lam_init)
        o_ref[:, h * LANES:(h + 1) * LANES] = o.astype(o_ref.dtype)


def _diff_attn(lam, qd, kd, vd, sub_g, lam_init, tq=512, tk=512, nh=DIFF_HEADS):
    T = qd.shape[0]
    W = nh * LANES
    whole = lambda h, i: (0, h)
    return pl.pallas_call(
        functools.partial(_diff_kernel, lam_init=lam_init, nh=nh, tk=tk),
        grid=(DIFF_HEADS // nh, T // tq),
        in_specs=[
            pl.BlockSpec(memory_space=pltpu.SMEM),
            pl.BlockSpec((tq, W), lambda h, i: (i, h)),
            pl.BlockSpec((T, W), whole, pipeline_mode=pl.Buffered(1)),
            pl.BlockSpec((T, W), whole, pipeline_mode=pl.Buffered(1)),
            pl.BlockSpec((1, LANES), lambda h, i: (0, 0)),
        ],
        out_specs=pl.BlockSpec((tq, W), lambda h, i: (i, h)),
        out_shape=jax.ShapeDtypeStruct((T, DIFF_HEADS * DIFF_V_DIM), BF16),
        scratch_shapes=[pltpu.VMEM((2 * nh, tq, 2 * LANES), F32)],
        compiler_params=_cparams(56, 2),
        name="diff_attention",
    )(lam, qd, kd, vd, sub_g)


def _from_token_tiles(v):
    v = jnp.swapaxes(v, 0, 1)
    return jnp.concatenate([v[s] for s in range(SUBLANES)], axis=1)


def _to_token_tiles(v):
    return jnp.swapaxes(jnp.stack([v[:, s * LANES:(s + 1) * LANES] for s in range(SUBLANES)], axis=0), 0, 1)


def _layer_norm(y, g, b):
    mu = jnp.mean(y, axis=-1, keepdims=True)
    d = y - mu
    var = jnp.mean(d * d, axis=-1, keepdims=True)
    return d * lax.rsqrt(var + NORM_EPS) * g + b


def _out_proj_kernel(oa_ref, ob_ref, x_ref, wo_ref, g_ref, b_ref, wrh_ref, wrl_ref, x1_ref, x1t_ref, logit_ref):
    half = oa_ref.shape[1]
    mix = jnp.dot(oa_ref[...], wo_ref[:half, :], preferred_element_type=F32)
    mix = mix + jnp.dot(ob_ref[...], wo_ref[half:, :], preferred_element_type=F32)
    x1 = _layer_norm(DEEPNORM_ALPHA * x_ref[...] + mix, g_ref[...], b_ref[...])
    x1_ref[...] = x1
    x1t_ref[...] = _to_token_tiles(x1)
    xh = x1.astype(BF16)
    xl = (x1 - xh.astype(F32)).astype(BF16)
    wh = wrh_ref[...]
    logits = jnp.dot(xh, wh, preferred_element_type=F32)
    logits = logits + jnp.dot(xl, wh, preferred_element_type=F32)
    logits = logits + jnp.dot(xh, wrl_ref[...], preferred_element_type=F32)
    logit_ref[...] = logits


def _out_proj(oa, ob, x2, w_out, g, b, wr_hi, wr_lo, tm=512):
    T = x2.shape[0]
    row = lambda i: (i, 0)
    fix = lambda i: (0, 0)
    return pl.pallas_call(
        _out_proj_kernel,
        grid=(T // tm,),
        in_specs=[
            pl.BlockSpec((tm, 512), row),
            pl.BlockSpec((tm, 512), row),
            pl.BlockSpec((tm, D_MODEL), row),
            pl.BlockSpec((D_MODEL, D_MODEL), fix),
            pl.BlockSpec((1, D_MODEL), fix),
            pl.BlockSpec((1, D_MODEL), fix),
            pl.BlockSpec((D_MODEL, N_EXPERTS), fix),
            pl.BlockSpec((D_MODEL, N_EXPERTS), fix),
        ],
        out_specs=[pl.BlockSpec((tm, D_MODEL), row), pl.BlockSpec((tm, SUBLANES, LANES), lambda i: (i, 0, 0)),
                   pl.BlockSpec((tm, N_EXPERTS), row)],
        out_shape=[jax.ShapeDtypeStruct((T, D_MODEL), F32), jax.ShapeDtypeStruct((T, SUBLANES, LANES), F32),
                   jax.ShapeDtypeStruct((T, N_EXPERTS), F32)],
        compiler_params=_cparams(48),
        name="out_proj_ln1_router",
    )(oa, ob, x2, w_out, g, b, wr_hi, wr_lo)


def _route_kernel(logit_ref, bias_ref, idx_ref, gate_ref, rank_ref, count_ref, run_ref):
    i = pl.program_id(0)
    tm = logit_ref.shape[0]
    scores = 1.0 / (1.0 + jnp.exp(-logit_ref[...]))
    sel = scores + bias_ref[...]
    lane = lax.broadcasted_iota(jnp.int32, (tm, N_EXPERTS), 1)
    lanef = lane.astype(F32)
    grp = lane // GROUP_SIZE
    neg_inf = -jnp.inf

    @pl.when(i == 0)
    def _():
        run_ref[...] = jnp.zeros_like(run_ref)

    def first_argmax(v):
        m = jnp.max(v, axis=-1, keepdims=True)
        idx = jnp.min(jnp.where(v == m, lanef, float(N_EXPERTS)), axis=-1, keepdims=True)
        return m, idx

    gscore = []
    for g in range(N_GROUPS):
        vg = jnp.where(grp == g, sel, neg_inf)
        m1, i1 = first_argmax(vg)
        m2 = jnp.max(jnp.where(lanef == i1, neg_inf, vg), axis=-1, keepdims=True)
        gscore.append(m1 + m2)
    keepf = jnp.zeros((tm, N_EXPERTS), F32)
    for g in range(N_GROUPS):
        ahead_n = jnp.zeros((tm, 1), F32)
        for o in range(N_GROUPS):
            if o == g:
                continue
            ahead = (gscore[o] > gscore[g]) if o > g else (gscore[o] >= gscore[g])
            ahead_n = ahead_n + jnp.where(ahead, 1.0, 0.0)
        keepf = jnp.where(grp == g, jnp.where(ahead_n < float(TOPK_GROUPS), 1.0, 0.0), keepf)
    cand = jnp.where(keepf > 0.5, sel, neg_inf)

    out_lane = lax.broadcasted_iota(jnp.int32, (tm, LANES), 1)
    idx_acc = jnp.zeros((tm, LANES), F32)
    sc_acc = jnp.zeros((tm, LANES), F32)
    hits = []
    chosen = jnp.zeros((tm, N_EXPERTS), F32)
    for k in range(TOP_K):
        _, ik = first_argmax(cand)
        hit = lanef == ik
        hits.append(hit)
        chosen = jnp.where(hit, 1.0, chosen)
        sk = jnp.sum(jnp.where(hit, scores, 0.0), axis=-1, keepdims=True)
        idx_acc = jnp.where(out_lane == k, ik, idx_acc)
        sc_acc = jnp.where(out_lane == k, sk, sc_acc)
        cand = jnp.where(hit, neg_inf, cand)
    total = jnp.sum(sc_acc, axis=-1, keepdims=True)
    idx_ref[...] = idx_acc.astype(jnp.int32)
    gate_ref[...] = sc_acc / total * ROUTED_SCALE

    r = lax.broadcasted_iota(jnp.int32, (tm, tm), 0)
    c = lax.broadcasted_iota(jnp.int32, (tm, tm), 1)
    lower = jnp.where(c < r, 1.0, 0.0).astype(BF16)
    before = jnp.dot(lower, chosen.astype(BF16), preferred_element_type=F32) + run_ref[...]
    rank_acc = jnp.zeros((tm, LANES), F32)
    for k in range(TOP_K):
        rk = jnp.sum(jnp.where(hits[k], before, 0.0), axis=-1, keepdims=True)
        rank_acc = jnp.where(out_lane == k, rk, rank_acc)
    rank_ref[...] = rank_acc.astype(jnp.int32)
    run_ref[...] = run_ref[...] + jnp.sum(chosen, axis=0, keepdims=True)
    count_ref[...] = run_ref[...].astype(jnp.int32)


def _route(logits, r_bias, tm=512):
    T = logits.shape[0]
    row = lambda i: (i, 0)
    fix = lambda i: (0, 0)
    return pl.pallas_call(
        _route_kernel,
        grid=(T // tm,),
        in_specs=[pl.BlockSpec((tm, N_EXPERTS), row), pl.BlockSpec((1, N_EXPERTS), fix)],
        out_specs=[pl.BlockSpec((tm, LANES), row), pl.BlockSpec((tm, LANES), row), pl.BlockSpec((tm, LANES), row),
                   pl.BlockSpec((1, N_EXPERTS), fix)],
        out_shape=[jax.ShapeDtypeStruct((T, LANES), jnp.int32), jax.ShapeDtypeStruct((T, LANES), F32),
                   jax.ShapeDtypeStruct((T, LANES), jnp.int32), jax.ShapeDtypeStruct((1, N_EXPERTS), jnp.int32)],
        scratch_shapes=[pltpu.VMEM((1, N_EXPERTS), F32)],
        compiler_params=_cparams(32),
        name="moe_route",
    )(logits, r_bias)


def _row_pos_kernel(idx_ref, rank_ref, pstart_ref, pos_ref):
    tm = idx_ref.shape[0]
    lanef = lax.broadcasted_iota(jnp.int32, (tm, N_EXPERTS), 1).astype(F32)
    out_lane = lax.broadcasted_iota(jnp.int32, (tm, LANES), 1)
    idx = idx_ref[...].astype(F32)
    pstart = pstart_ref[...]
    base = jnp.zeros((tm, LANES), F32)
    for k in range(TOP_K):
        ik = idx[:, k:k + 1]
        pk = jnp.sum(jnp.where(lanef == ik, pstart, 0.0), axis=-1, keepdims=True)
        base = jnp.where(out_lane == k, pk, base)
    pos_ref[...] = base.astype(jnp.int32) + rank_ref[...]


def _row_pos(idx, rank, pstarts, tm=512):
    T = idx.shape[0]
    row = lambda i: (i, 0)
    return pl.pallas_call(
        _row_pos_kernel,
        grid=(T // tm,),
        in_specs=[pl.BlockSpec((tm, LANES), row), pl.BlockSpec((tm, LANES), row),
                  pl.BlockSpec((1, N_EXPERTS), lambda i: (0, 0))],
        out_specs=pl.BlockSpec((tm, LANES), row),
        out_shape=jax.ShapeDtypeStruct((T, LANES), jnp.int32),
        compiler_params=_cparams(32),
        name="moe_row_pos",
    )(idx, rank, pstarts)


GATHER_UNROLL = 8


def _start_row_gather(idx_ref, n_rows, src_hbm, dst, row0, sem, unrolled=False, idx0=0):
    ids = idx_ref.at[0, 0, pl.ds(idx0, n_rows)]

    def start(r, priority):
        t = ids[r]
        pltpu.make_async_copy(src_hbm.at[t], dst.at[row0 + r], sem).start(priority=priority)

    if unrolled:
        for r in range(n_rows):
            start(r, r % 2)
    else:
        def body(g, carry):
            for j in range(GATHER_UNROLL):
                start(g * GATHER_UNROLL + j, j % 2)
            return carry
        lax.fori_loop(0, n_rows // GATHER_UNROLL, body, 0)


def _wait_row_gather(n_rows, src_hbm, dst, row0, sem):
    pltpu.make_async_copy(src_hbm.at[pl.ds(0, n_rows)], dst.at[pl.ds(row0, n_rows)], sem).wait()


GATHER_SLOTS = 3
BLOCK_GROUP = 8


def _expert_kernel(blk_ref, e_ref, lo_ref, hi_ref, tok_ref, gate_ref, x_hbm, wg_ref, wu_ref, wd_ref,
                   y_ref, buf, sem, wgb, wub, wdb):
    i = pl.program_id(0)
    n = pl.num_programs(0)
    rows = MOE_BLOCK

    def in_group(d):
        return (blk_ref[jnp.minimum(i + d, n - 1)] % BLOCK_GROUP) * rows

    @pl.when(i == 0)
    def _():
        _start_row_gather(tok_ref, rows, x_hbm, buf, 0, sem.at[0], idx0=in_group(0))
        _start_row_gather(tok_ref, rows, x_hbm, buf, rows, sem.at[1], idx0=in_group(1))

    @pl.when(jnp.logical_or(i == 0, blk_ref[i] != blk_ref[jnp.maximum(i - 1, 0)]))
    def _():
        y_ref[...] = jnp.zeros_like(y_ref)

    @pl.when(jnp.logical_or(i == 0, e_ref[i] != e_ref[jnp.maximum(i - 1, 0)]))
    def _():
        wgb[...] = wg_ref[...].astype(BF16)
        wub[...] = wu_ref[...].astype(BF16)
        wdb[...] = wd_ref[...].astype(BF16)

    slot = i % GATHER_SLOTS
    row0 = pl.multiple_of(slot * rows, rows)
    _wait_row_gather(rows, x_hbm, buf, row0, sem.at[slot])
    xg = _from_token_tiles(buf[pl.ds(row0, rows)]).astype(BF16)
    nslot = (i + 2) % GATHER_SLOTS
    _start_row_gather(tok_ref, rows, x_hbm, buf, nslot * rows, sem.at[nslot], unrolled=True, idx0=in_group(2))
    a = jnp.dot(xg, wgb[...], preferred_element_type=F32)
    u = jnp.dot(xg, wub[...], preferred_element_type=F32)
    h = (a / (1.0 + jnp.exp(-a)) * u).astype(BF16)
    y = jnp.dot(h, wdb[...], preferred_element_type=F32)
    r = lax.broadcasted_iota(jnp.int32, (rows, 1), 0)
    mine = jnp.logical_and(r >= lo_ref[i], r < hi_ref[i])
    grow = gate_ref[0, pl.ds(blk_ref[i] % BLOCK_GROUP, 1), :]
    eye = lax.broadcasted_iota(jnp.int32, (rows, rows), 0) == lax.broadcasted_iota(jnp.int32, (rows, rows), 1)
    gate = jnp.sum(jnp.where(eye, grow, 0.0), axis=1, keepdims=True)
    y_ref[...] += _to_token_tiles(jnp.where(mine, y * gate, 0.0))

    @pl.when(i == n - 1)
    def _():
        for d in (1, 2):
            s = (i + d) % GATHER_SLOTS
            _wait_row_gather(rows, x_hbm, buf, s * rows, sem.at[s])


def _experts(items, tok_blocks, gate_blocks, x1, w_gate, w_up, w_down):
    it_blk, it_e, it_lo, it_hi = items
    NI = it_blk.shape[0]
    NBLK = tok_blocks.shape[0] * BLOCK_GROUP
    ahead = lambda d: (lambda i, blk, e, lo, hi: (blk[jnp.minimum(i + d, NI - 1)] // BLOCK_GROUP, 0, 0))
    wsel = lambda i, blk, e, lo, hi: (e[i], 0, 0)
    group_rows = BLOCK_GROUP * MOE_BLOCK
    grid_spec = pltpu.PrefetchScalarGridSpec(
        num_scalar_prefetch=4,
        grid=(NI,),
        in_specs=[
            pl.BlockSpec((1, 1, group_rows), ahead(2), memory_space=pltpu.SMEM),
            pl.BlockSpec((1, BLOCK_GROUP, MOE_BLOCK), ahead(0)),
            pl.BlockSpec(memory_space=pl.ANY),
            pl.BlockSpec((None, D_MODEL, EXPERT_FF), wsel),
            pl.BlockSpec((None, D_MODEL, EXPERT_FF), wsel),
            pl.BlockSpec((None, EXPERT_FF, D_MODEL), wsel),
        ],
        out_specs=pl.BlockSpec((MOE_BLOCK, SUBLANES, LANES), lambda i, blk, e, lo, hi: (blk[i], 0, 0)),
        scratch_shapes=[pltpu.VMEM((GATHER_SLOTS * MOE_BLOCK, SUBLANES, LANES), F32),
                        pltpu.SemaphoreType.DMA((GATHER_SLOTS,)),
                        pltpu.VMEM((D_MODEL, EXPERT_FF), BF16), pltpu.VMEM((D_MODEL, EXPERT_FF), BF16),
                        pltpu.VMEM((EXPERT_FF, D_MODEL), BF16)],
    )
    return pl.pallas_call(
        _expert_kernel,
        grid_spec=grid_spec,
        out_shape=jax.ShapeDtypeStruct((NBLK * MOE_BLOCK, SUBLANES, LANES), F32),
        compiler_params=_cparams(48),
        name="moe_experts",
    )(it_blk, it_e, it_lo, it_hi, tok_blocks, gate_blocks, x1, w_gate, w_up, w_down)


def _combine_kernel(posc_ref, posn_ref, ys_hbm, x1_ref, wsg_ref, wsu_ref, wsd_ref, g_ref, b_ref, out_ref, buf, sem):
    i = pl.program_id(0)
    n = pl.num_programs(0)
    tt = x1_ref.shape[0]
    rows = TOP_K * tt
    slot = i % 2

    @pl.when(i == 0)
    def _():
        _start_row_gather(posc_ref, rows, ys_hbm, buf, 0, sem.at[0])

    @pl.when(i + 1 < n)
    def _():
        nslot = (i + 1) % 2
        _start_row_gather(posn_ref, rows, ys_hbm, buf, nslot * rows, sem.at[nslot])

    x1 = x1_ref[...]
    xb = x1.astype(BF16)
    a = jnp.dot(xb, wsg_ref[...], preferred_element_type=F32)
    u = jnp.dot(xb, wsu_ref[...], preferred_element_type=F32)
    h = (a / (1.0 + jnp.exp(-a)) * u).astype(BF16)
    ffn = jnp.dot(h, wsd_ref[...], preferred_element_type=F32)

    row0 = pl.multiple_of(slot * rows, rows)
    _wait_row_gather(rows, ys_hbm, buf, row0, sem.at[slot])
    routed = buf[pl.ds(row0, tt)]
    for k in range(1, TOP_K):
        routed = routed + buf[pl.ds(row0 + k * tt, tt)]
    ffn = ffn + _from_token_tiles(routed)
    out_ref[...] = _layer_norm(DEEPNORM_ALPHA * x1 + ffn, g_ref[...], b_ref[...])


def _combine(pos_blocks, ys, x1, ws_gate, ws_up, ws_down, g, b, tt=128):
    T = x1.shape[0]
    n = T // tt
    rows = TOP_K * tt
    row = lambda i: (i, 0)
    fix = lambda i: (0, 0)
    return pl.pallas_call(
        _combine_kernel,
        grid=(n,),
        in_specs=[
            pl.BlockSpec((1, 1, rows), lambda i: (i, 0, 0), memory_space=pltpu.SMEM),
            pl.BlockSpec((1, 1, rows), lambda i: (jnp.minimum(i + 1, n - 1), 0, 0), memory_space=pltpu.SMEM),
            pl.BlockSpec(memory_space=pl.ANY),
            pl.BlockSpec((tt, D_MODEL), row),
            pl.BlockSpec((D_MODEL, SHARED_FF), fix),
            pl.BlockSpec((D_MODEL, SHARED_FF), fix),
            pl.BlockSpec((SHARED_FF, D_MODEL), fix),
            pl.BlockSpec((1, D_MODEL), fix),
            pl.BlockSpec((1, D_MODEL), fix),
        ],
        out_specs=pl.BlockSpec((tt, D_MODEL), row),
        out_shape=jax.ShapeDtypeStruct((T, D_MODEL), F32),
        scratch_shapes=[pltpu.VMEM((2 * rows, SUBLANES, LANES), F32), pltpu.SemaphoreType.DMA((2,))],
        compiler_params=_cparams(56),
        name="moe_combine_ln2",
    )(pos_blocks, pos_blocks, ys, x1, ws_gate, ws_up, ws_down, g, b)


def _rope_tables(T):
    inv = 1.0 / (ROPE_THETA ** (jnp.arange(0, HEAD_DIM, 2, dtype=F32) / HEAD_DIM))
    ang = jnp.arange(T, dtype=F32)[:, None] * inv[None, :]
    cos, sin = jnp.cos(ang), jnp.sin(ang)
    cos128 = jnp.tile(cos, (1, 4))
    sin128 = jnp.tile(jnp.concatenate([-sin, sin], axis=1), (1, 2))
    return cos128, sin128


def _swa_head_order(a, axis):
    half = SWA_Q_HEADS // 2
    shp = a.shape
    a = a.reshape(shp[:axis] + (2, half, HEAD_DIM) + shp[axis + 1:])
    return jnp.swapaxes(a, axis, axis + 1).reshape(shp)


def _item_tables(top_e, gate, counts, T):
    A = T * TOP_K
    NBLK = A // MOE_BLOCK
    NI = NBLK + N_EXPERTS - 1
    starts = jnp.cumsum(counts) - counts
    ends = starts + counts
    first_blk = starts // MOE_BLOCK
    nblk = jnp.where(counts > 0, (ends - 1) // MOE_BLOCK - first_blk + 1, 0)
    item_end = jnp.cumsum(nblk)
    item_off = item_end - nblk
    it = jnp.arange(NI, dtype=jnp.int32)
    live = it < item_end[-1]
    it_e = jnp.minimum(jnp.sum((item_end[None, :] <= it[:, None]).astype(jnp.int32), axis=1), N_EXPERTS - 1)
    onehot = (it_e[:, None] == jnp.arange(N_EXPERTS, dtype=jnp.int32)[None, :]).astype(jnp.int32)
    pick = lambda v: jnp.sum(onehot * v[None, :].astype(jnp.int32), axis=1)
    it_blk = pick(first_blk) + it - pick(item_off)
    it_lo = jnp.clip(pick(starts) - it_blk * MOE_BLOCK, 0, MOE_BLOCK)
    it_hi = jnp.clip(pick(ends) - it_blk * MOE_BLOCK, 0, MOE_BLOCK)
    it_blk = jnp.where(live, it_blk, NBLK - 1)
    it_lo = jnp.where(live, it_lo, 0)
    it_hi = jnp.where(live, it_hi, 0)
    tok = jnp.arange(T, dtype=jnp.int32)[:, None]
    key_s, gate_s = lax.sort(((top_e * T + tok).reshape(A), gate.reshape(A)), num_keys=1)
    items = tuple(v.astype(jnp.int32) for v in (it_blk, it_e, it_lo, it_hi))
    group_rows = BLOCK_GROUP * MOE_BLOCK
    return (items, (key_s % T).reshape(A // group_rows, 1, group_rows),
            gate_s.reshape(A // group_rows, BLOCK_GROUP, MOE_BLOCK), starts)


def kernel(x, w_in, b_in, sinks, lam_q1, lam_k1, lam_q2, lam_k2, diff_norm_g, w_out, ln1_g, ln1_b,
           w_router, router_bias, w_gate, w_up, w_down, ws_gate, ws_up, ws_down, ln2_g, ln2_b):
    B, S, D = x.shape
    assert D == D_MODEL and w_in.shape[0] == DEPTH
    T = B * S
    assert B == 1 and T % 512 == 0
    cos128, sin128 = _rope_tables(S)
    x2 = x.reshape(T, D)
    for l in range(DEPTH):
        lam_init = 0.8 - 0.6 * math.exp(-0.3 * l)
        wi = w_in[l]
        wi = jnp.concatenate([_swa_head_order(wi[:, :_AK], 1), wi[:, _AK:]], axis=1).astype(BF16)
        bi = jnp.concatenate([_swa_head_order(b_in[l][:_AK], 0), b_in[l][_AK:]])[None, :]
        wo = jnp.concatenate([_swa_head_order(w_out[l][:_AK], 0), w_out[l][_AK:]], axis=0).astype(BF16)
        sink_p = sinks[l].astype(F32)
        lam = (jnp.exp(jnp.sum(lam_q1[l].astype(F32) * lam_k1[l].astype(F32)))
               - jnp.exp(jnp.sum(lam_q2[l].astype(F32) * lam_k2[l].astype(F32))) + lam_init).reshape(1)
        wr = w_router[l]
        wr_hi = wr.astype(BF16)
        wr_lo = (wr - wr_hi.astype(F32)).astype(BF16)

        qa, ka, va, qd, kd, vd = _in_proj(x2, wi, bi, cos128, sin128)
        oa = _swa(sink_p, qa, ka, va)
        ob = _diff_attn(lam, qd, kd, vd, diff_norm_g[l][None, :], lam_init)
        x1, x1t, logits = _out_proj(oa, ob, x2, wo, ln1_g[l][None, :], ln1_b[l][None, :], wr_hi, wr_lo)
        idx, gates, rank, counts = _route(logits, router_bias[l][None, :])
        items, tok_blocks, gate_blocks, starts = _item_tables(idx[:, :TOP_K], gates[:, :TOP_K], counts[0], T)
        pos = _row_pos(idx, rank, starts.astype(F32)[None, :])[:, :TOP_K]
        ys = _experts(items, tok_blocks, gate_blocks, x1t, w_gate[l], w_up[l], w_down[l])
        tt = 512
        pos_blocks = pos.reshape(T // tt, tt, TOP_K).transpose(0, 2, 1).reshape(T // tt, 1, TOP_K * tt)
        x2 = _combine(pos_blocks, ys, x1, ws_gate[l].astype(BF16), ws_up[l].astype(BF16),
                      ws_down[l].astype(BF16), ln2_g[l][None, :], ln2_b[l][None, :], tt=tt)
    return x2.reshape(B, S, D)
```

```python
import functools
import math

import numpy as np
import jax
import jax.numpy as jnp
from jax import lax
from jax.experimental import pallas as pl
from jax.experimental.pallas import tpu as pltpu

D_MODEL = 1024
CHUNK = 64
HEAD_DIM = 64
ROPE_THETA = 10000.0
SWA_Q_HEADS = 8
SWA_KV_HEADS = 2
SWA_WIN_CHUNKS = 2
DIFF_HEADS = 4
DIFF_V_DIM = 2 * HEAD_DIM
N_EXPERTS = 256
TOP_K = 8
N_GROUPS = 8
TOPK_GROUPS = 4
GROUP_SIZE = N_EXPERTS // N_GROUPS
EXPERT_FF = 256
SHARED_FF = 256
ROUTED_SCALE = 2.5
MOE_BLOCK = 256
DEPTH = 1
DEEPNORM_ALPHA = (2.0 * DEPTH) ** 0.25
NORM_EPS = 1e-5
LOG2E = math.log2(math.e)
NEG_BIG = -1e30
LANES = 128
SUBLANES = 8

_AQ, _AK, _AV, _BQ, _BK, _BV, _END = 0, 512, 640, 768, 1280, 1792, 2304
IN_WIDTH = _END

_NT = (((1,), (1,)), ((), ()))

F32 = jnp.float32
BF16 = jnp.bfloat16


def _cparams(vmem_mb, n_axes=1):
    return pltpu.CompilerParams(dimension_semantics=("arbitrary",) * n_axes,
                                vmem_limit_bytes=vmem_mb * 1024 * 1024)


def _in_proj_kernel(x_ref, w_ref, b_ref, cos_ref, sin_ref,
                    qa_ref, ka_ref, va_ref, qd_ref, kd_ref, vd_ref):
    tm = x_ref.shape[0]
    x = x_ref[...].astype(BF16)
    cos = cos_ref[...]
    sin = sin_ref[...]
    lane = lax.broadcasted_iota(jnp.int32, (tm, LANES), 1)
    first_half = (lane % HEAD_DIM) < (HEAD_DIM // 2)
    q_scale = HEAD_DIM ** -0.5 * LOG2E

    def proj(c0, c1):
        return jnp.dot(x, w_ref[:, c0:c1], preferred_element_type=F32) + b_ref[:, c0:c1]

    def rope(c):
        rot = jnp.where(first_half, pltpu.roll(c, LANES - 32, 1), pltpu.roll(c, 32, 1))
        return c * cos + rot * sin

    def emit(out_ref, c0, c1, rotary, scale):
        for g0 in range(c0, c1, 256):
            g1 = min(g0 + 256, c1)
            p = proj(g0, g1)
            for j in range(0, g1 - g0, LANES):
                c = p[:, j:j + LANES]
                if rotary:
                    c = rope(c)
                if scale != 1.0:
                    c = c * scale
                out_ref[:, g0 - c0 + j:g0 - c0 + j + LANES] = c.astype(out_ref.dtype)

    emit(qa_ref, _AQ, _AK, True, q_scale)
    emit(ka_ref, _AK, _AV, True, 1.0)
    emit(va_ref, _AV, _BQ, False, 1.0)
    emit(qd_ref, _BQ, _BK, True, q_scale)
    emit(kd_ref, _BK, _BV, True, 1.0)
    emit(vd_ref, _BV, _END, False, 1.0)


def _in_proj(x2, w_in, b_in, cos128, sin128, tm=512):
    T = x2.shape[0]
    outs = [(T, 512), (T, 128), (T, 128), (T, 512), (T, 512), (T, 512)]
    return pl.pallas_call(
        _in_proj_kernel,
        grid=(T // tm,),
        in_specs=[
            pl.BlockSpec((tm, D_MODEL), lambda i: (i, 0)),
            pl.BlockSpec((D_MODEL, IN_WIDTH), lambda i: (0, 0)),
            pl.BlockSpec((1, IN_WIDTH), lambda i: (0, 0)),
            pl.BlockSpec((tm, LANES), lambda i: (i, 0)),
            pl.BlockSpec((tm, LANES), lambda i: (i, 0)),
        ],
        out_specs=[pl.BlockSpec((tm, s[1]), lambda i: (i, 0)) for s in outs],
        out_shape=[jax.ShapeDtypeStruct(s, BF16) for s in outs],
        compiler_params=_cparams(48),
        name="in_proj_rope",
    )(x2, w_in, b_in, cos128, sin128)


def _swa_kernel(sink_ref, q_ref, kp_ref, kc_ref, vp_ref, vc_ref, o_ref):
    i = pl.program_id(0)
    tq = q_ref.shape[0]
    back = SWA_WIN_CHUNKS * CHUNK
    keys = jnp.concatenate([kp_ref[tq - back:, :], kc_ref[...]], axis=0)
    vals = jnp.concatenate([vp_ref[tq - back:, :], vc_ref[...]], axis=0)
    nk = tq + back
    qc = lax.broadcasted_iota(jnp.int32, (tq, nk), 0) // CHUNK
    kc = lax.broadcasted_iota(jnp.int32, (tq, nk), 1) // CHUNK - SWA_WIN_CHUNKS
    lo = jnp.where(i > 0, qc - SWA_WIN_CHUNKS, jnp.maximum(qc - SWA_WIN_CHUNKS, 0))
    bias = jnp.where(kc <= qc, jnp.where(kc >= lo, 0.0, NEG_BIG), NEG_BIG).astype(F32)
    lane = lax.broadcasted_iota(jnp.int32, (tq, LANES), 1)
    low = lane < HEAD_DIM
    zero = jnp.zeros((), BF16)
    for c in range(SWA_Q_HEADS // 2):
        qch = q_ref[:, c * LANES:(c + 1) * LANES]
        outs = []
        for half in range(2):
            h = c + (SWA_Q_HEADS // 2) * half
            qm = jnp.where(low if half == 0 else jnp.logical_not(low), qch, zero)
            s = lax.dot_general(qm, keys, _NT, preferred_element_type=F32) + bias
            sink = sink_ref[h] * LOG2E
            m = jnp.maximum(jnp.max(s, axis=-1, keepdims=True), sink)
            p = jnp.exp2(s - m)
            denom = jnp.sum(p, axis=-1, keepdims=True) + jnp.exp2(sink - m)
            o = jnp.dot(p.astype(BF16), vals, preferred_element_type=F32)
            outs.append(o / denom)
        o_ref[:, c * LANES:(c + 1) * LANES] = jnp.where(low, outs[0], outs[1]).astype(o_ref.dtype)


def _swa(sinks, qa, ka, va, tq=256):
    T = qa.shape[0]
    prev = lambda i: (jnp.maximum(i - 1, 0), 0)
    cur = lambda i: (i, 0)
    return pl.pallas_call(
        _swa_kernel,
        grid=(T // tq,),
        in_specs=[
            pl.BlockSpec(memory_space=pltpu.SMEM),
            pl.BlockSpec((tq, 512), cur),
            pl.BlockSpec((tq, LANES), prev),
            pl.BlockSpec((tq, LANES), cur),
            pl.BlockSpec((tq, LANES), prev),
            pl.BlockSpec((tq, LANES), cur),
        ],
        out_specs=pl.BlockSpec((tq, 512), cur),
        out_shape=jax.ShapeDtypeStruct((T, 512), BF16),
        compiler_params=_cparams(32),
        name="swa_sink_attention",
    )(sinks, qa, ka, ka, va, va)


KV_TILES_PER_TRIP = 4


def _diff_kernel(lam_ref, q_ref, k_ref, v_ref, g_ref, o_ref, acc_ref, *, lam_init, nh, tk):
    i = pl.program_id(1)
    tq = q_ref.shape[0]
    lane = lax.broadcasted_iota(jnp.int32, (tq, LANES), 1)
    zero = jnp.zeros((), BF16)
    qs = []
    for h in range(nh):
        q = q_ref[:, h * LANES:(h + 1) * LANES]
        qs.append(jnp.where(lane < HEAD_DIM, q, zero))
        qs.append(jnp.where(lane >= HEAD_DIM, q, zero))
    ones_col = jnp.where(lax.broadcasted_iota(jnp.int32, (tk, LANES), 1) == 0, 1.0, 0.0).astype(BF16)
    acc_ref[...] = jnp.zeros_like(acc_ref)

    def step(j, carry, diagonal):
        start = pl.multiple_of(j * tk, tk)
        new = []
        if diagonal:
            rc = (lax.broadcasted_iota(jnp.int32, (tq, tk), 0) + i * tq) // CHUNK
            cc = (lax.broadcasted_iota(jnp.int32, (tq, tk), 1) + j * tk) // CHUNK
            visible = cc <= rc
        for h in range(nh):
            k = k_ref[pl.ds(start, tk), h * LANES:(h + 1) * LANES]
            v = v_ref[pl.ds(start, tk), h * LANES:(h + 1) * LANES]
            vext = jnp.concatenate([v, ones_col], axis=1)
            for mi in range(2):
                c = 2 * h + mi
                s = lax.dot_general(qs[c], k, _NT, preferred_element_type=F32)
                if diagonal:
                    s = jnp.where(visible, s, NEG_BIG)
                m_old = carry[c]
                m_new = jnp.maximum(m_old, jnp.max(s, axis=-1, keepdims=True))
                alpha = jnp.exp2(m_old - m_new)
                p = jnp.exp2((s - m_new).astype(BF16))
                acc_ref[c] = acc_ref[c] * alpha + jnp.dot(p, vext, preferred_element_type=F32)
                new.append(m_new)
        return tuple(new)

    init = tuple(jnp.full((tq, 1), NEG_BIG, F32) for _ in range(2 * nh))
    nfull = i * (tq // tk)
    def trip(jj, c):
        for t in range(KV_TILES_PER_TRIP):
            c = step(KV_TILES_PER_TRIP * jj + t, c, False)
        return c

    ntrip = nfull // KV_TILES_PER_TRIP
    carry = lax.fori_loop(0, ntrip, trip, init)
    carry = lax.fori_loop(KV_TILES_PER_TRIP * ntrip, nfull, lambda j, c: step(j, c, False), carry)
    for d in range(tq // tk):
        carry = step(nfull + d, carry, True)
    lam = lam_ref[0]
    for h in range(nh):
        a0 = acc_ref[2 * h]
        a1 = acc_ref[2 * h + 1]
        o = (a0[:, :DIFF_V_DIM] / a0[:, DIFF_V_DIM:DIFF_V_DIM + 1]
             - lam * (a1[:, :DIFF_V_DIM] / a1[:, DIFF_V_DIM:DIFF_V_DIM + 1]))
        ms = jnp.mean(o * o, axis=-1, keepdims=True)
        o = o * lax.rsqrt(ms + NORM_EPS) * g_ref[...] * (1.0 - lam_init)
        o_ref[:, h * LANES:(h + 1) * LANES] = o.astype(o_ref.dtype)


def _diff_attn(lam, qd, kd, vd, sub_g, lam_init, tq=512, tk=512, nh=DIFF_HEADS):
    T = qd.shape[0]
    W = nh * LANES
    whole = lambda h, i: (0, h)
    return pl.pallas_call(
        functools.partial(_diff_kernel, lam_init=lam_init, nh=nh, tk=tk),
        grid=(DIFF_HEADS // nh, T // tq),
        in_specs=[
            pl.BlockSpec(memory_space=pltpu.SMEM),
            pl.BlockSpec((tq, W), lambda h, i: (i, h)),
            pl.BlockSpec((T, W), whole, pipeline_mode=pl.Buffered(1)),
            pl.BlockSpec((T, W), whole, pipeline_mode=pl.Buffered(1)),
            pl.BlockSpec((1, LANES), lambda h, i: (0, 0)),
        ],
        out_specs=pl.BlockSpec((tq, W), lambda h, i: (i, h)),
        out_shape=jax.ShapeDtypeStruct((T, DIFF_HEADS * DIFF_V_DIM), BF16),
        scratch_shapes=[pltpu.VMEM((2 * nh, tq, 2 * LANES), F32)],
        compiler_params=_cparams(56, 2),
        name="diff_attention",
    )(lam, qd, kd, vd, sub_g)


def _from_token_tiles(v):
    v = jnp.swapaxes(v, 0, 1)
    return jnp.concatenate([v[s] for s in range(SUBLANES)], axis=1)


def _to_token_tiles(v):
    return jnp.swapaxes(jnp.stack([v[:, s * LANES:(s + 1) * LANES] for s in range(SUBLANES)], axis=0), 0, 1)


def _layer_norm(y, g, b):
    mu = jnp.mean(y, axis=-1, keepdims=True)
    d = y - mu
    var = jnp.mean(d * d, axis=-1, keepdims=True)
    return d * lax.rsqrt(var + NORM_EPS) * g + b


def _out_proj_kernel(oa_ref, ob_ref, x_ref, wo_ref, g_ref, b_ref, wrh_ref, wrl_ref, x1_ref, x1t_ref, logit_ref):
    half = oa_ref.shape[1]
    mix = jnp.dot(oa_ref[...], wo_ref[:half, :], preferred_element_type=F32)
    mix = mix + jnp.dot(ob_ref[...], wo_ref[half:, :], preferred_element_type=F32)
    x1 = _layer_norm(DEEPNORM_ALPHA * x_ref[...] + mix, g_ref[...], b_ref[...])
    x1_ref[...] = x1
    x1t_ref[...] = _to_token_tiles(x1)
    xh = x1.astype(BF16)
    xl = (x1 - xh.astype(F32)).astype(BF16)
    wh = wrh_ref[...]
    logits = jnp.dot(xh, wh, preferred_element_type=F32)
    logits = logits + jnp.dot(xl, wh, preferred_element_type=F32)
    logits = logits + jnp.dot(xh, wrl_ref[...], preferred_element_type=F32)
    logit_ref[...] = logits


def _out_proj(oa, ob, x2, w_out, g, b, wr_hi, wr_lo, tm=512):
    T = x2.shape[0]
    row = lambda i: (i, 0)
    fix = lambda i: (0, 0)
    return pl.pallas_call(
        _out_proj_kernel,
        grid=(T // tm,),
        in_specs=[
            pl.BlockSpec((tm, 512), row),
            pl.BlockSpec((tm, 512), row),
            pl.BlockSpec((tm, D_MODEL), row),
            pl.BlockSpec((D_MODEL, D_MODEL), fix),
            pl.BlockSpec((1, D_MODEL), fix),
            pl.BlockSpec((1, D_MODEL), fix),
            pl.BlockSpec((D_MODEL, N_EXPERTS), fix),
            pl.BlockSpec((D_MODEL, N_EXPERTS), fix),
        ],
        out_specs=[pl.BlockSpec((tm, D_MODEL), row), pl.BlockSpec((tm, SUBLANES, LANES), lambda i: (i, 0, 0)),
                   pl.BlockSpec((tm, N_EXPERTS), row)],
        out_shape=[jax.ShapeDtypeStruct((T, D_MODEL), F32), jax.ShapeDtypeStruct((T, SUBLANES, LANES), F32),
                   jax.ShapeDtypeStruct((T, N_EXPERTS), F32)],
        compiler_params=_cparams(48),
        name="out_proj_ln1_router",
    )(oa, ob, x2, w_out, g, b, wr_hi, wr_lo)


def _route_kernel(logit_ref, bias_ref, idx_ref, gate_ref, rank_ref, count_ref, run_ref):
    i = pl.program_id(0)
    tm = logit_ref.shape[0]
    scores = 1.0 / (1.0 + jnp.exp(-logit_ref[...]))
    sel = scores + bias_ref[...]
    lane = lax.broadcasted_iota(jnp.int32, (tm, N_EXPERTS), 1)
    lanef = lane.astype(F32)
    grp = lane // GROUP_SIZE
    neg_inf = -jnp.inf

    @pl.when(i == 0)
    def _():
        run_ref[...] = jnp.zeros_like(run_ref)

    def first_argmax(v):
        m = jnp.max(v, axis=-1, keepdims=True)
        idx = jnp.min(jnp.where(v == m, lanef, float(N_EXPERTS)), axis=-1, keepdims=True)
        return m, idx

    gscore = []
    for g in range(N_GROUPS):
        vg = jnp.where(grp == g, sel, neg_inf)
        m1, i1 = first_argmax(vg)
        m2 = jnp.max(jnp.where(lanef == i1, neg_inf, vg), axis=-1, keepdims=True)
        gscore.append(m1 + m2)
    keepf = jnp.zeros((tm, N_EXPERTS), F32)
    for g in range(N_GROUPS):
        ahead_n = jnp.zeros((tm, 1), F32)
        for o in range(N_GROUPS):
            if o == g:
                continue
            ahead = (gscore[o] > gscore[g]) if o > g else (gscore[o] >= gscore[g])
            ahead_n = ahead_n + jnp.where(ahead, 1.0, 0.0)
        keepf = jnp.where(grp == g, jnp.where(ahead_n < float(TOPK_GROUPS), 1.0, 0.0), keepf)
    cand = jnp.where(keepf > 0.5, sel, neg_inf)

    out_lane = lax.broadcasted_iota(jnp.int32, (tm, LANES), 1)
    idx_acc = jnp.zeros((tm, LANES), F32)
    sc_acc = jnp.zeros((tm, LANES), F32)
    hits = []
    chosen = jnp.zeros((tm, N_EXPERTS), F32)
    for k in range(TOP_K):
        _, ik = first_argmax(cand)
        hit = lanef == ik
        hits.append(hit)
        chosen = jnp.where(hit, 1.0, chosen)
        sk = jnp.sum(jnp.where(hit, scores, 0.0), axis=-1, keepdims=True)
        idx_acc = jnp.where(out_lane == k, ik, idx_acc)
        sc_acc = jnp.where(out_lane == k, sk, sc_acc)
        cand = jnp.where(hit, neg_inf, cand)
    total = jnp.sum(sc_acc, axis=-1, keepdims=True)
    idx_ref[...] = idx_acc.astype(jnp.int32)
    gate_ref[...] = sc_acc / total * ROUTED_SCALE

    r = lax.broadcasted_iota(jnp.int32, (tm, tm), 0)
    c = lax.broadcasted_iota(jnp.int32, (tm, tm), 1)
    lower = jnp.where(c < r, 1.0, 0.0).astype(BF16)
    before = jnp.dot(lower, chosen.astype(BF16), preferred_element_type=F32) + run_ref[...]
    rank_acc = jnp.zeros((tm, LANES), F32)
    for k in range(TOP_K):
        rk = jnp.sum(jnp.where(hits[k], before, 0.0), axis=-1, keepdims=True)
        rank_acc = jnp.where(out_lane == k, rk, rank_acc)
    rank_ref[...] = rank_acc.astype(jnp.int32)
    run_ref[...] = run_ref[...] + jnp.sum(chosen, axis=0, keepdims=True)
    count_ref[...] = run_ref[...].astype(jnp.int32)


def _route(logits, r_bias, tm=512):
    T = logits.shape[0]
    row = lambda i: (i, 0)
    fix = lambda i: (0, 0)
    return pl.pallas_call(
        _route_kernel,
        grid=(T // tm,),
        in_specs=[pl.BlockSpec((tm, N_EXPERTS), row), pl.BlockSpec((1, N_EXPERTS), fix)],
        out_specs=[pl.BlockSpec((tm, LANES), row), pl.BlockSpec((tm, LANES), row), pl.BlockSpec((tm, LANES), row),
                   pl.BlockSpec((1, N_EXPERTS), fix)],
        out_shape=[jax.ShapeDtypeStruct((T, LANES), jnp.int32), jax.ShapeDtypeStruct((T, LANES), F32),
                   jax.ShapeDtypeStruct((T, LANES), jnp.int32), jax.ShapeDtypeStruct((1, N_EXPERTS), jnp.int32)],
        scratch_shapes=[pltpu.VMEM((1, N_EXPERTS), F32)],
        compiler_params=_cparams(32),
        name="moe_route",
    )(logits, r_bias)


def _row_pos_kernel(idx_ref, rank_ref, pstart_ref, pos_ref):
    tm = idx_ref.shape[0]
    lanef = lax.broadcasted_iota(jnp.int32, (tm, N_EXPERTS), 1).astype(F32)
    out_lane = lax.broadcasted_iota(jnp.int32, (tm, LANES), 1)
    idx = idx_ref[...].astype(F32)
    pstart = pstart_ref[...]
    base = jnp.zeros((tm, LANES), F32)
    for k in range(TOP_K):
        ik = idx[:, k:k + 1]
        pk = jnp.sum(jnp.where(lanef == ik, pstart, 0.0), axis=-1, keepdims=True)
        base = jnp.where(out_lane == k, pk, base)
    pos_ref[...] = base.astype(jnp.int32) + rank_ref[...]


def _row_pos(idx, rank, pstarts, tm=512):
    T = idx.shape[0]
    row = lambda i: (i, 0)
    return pl.pallas_call(
        _row_pos_kernel,
        grid=(T // tm,),
        in_specs=[pl.BlockSpec((tm, LANES), row), pl.BlockSpec((tm, LANES), row),
                  pl.BlockSpec((1, N_EXPERTS), lambda i: (0, 0))],
        out_specs=pl.BlockSpec((tm, LANES), row),
        out_shape=jax.ShapeDtypeStruct((T, LANES), jnp.int32),
        compiler_params=_cparams(32),
        name="moe_row_pos",
    )(idx, rank, pstarts)


GATHER_UNROLL = 8


def _start_row_gather(idx_ref, n_rows, src_hbm, dst, row0, sem, unrolled=False, idx0=0):
    ids = idx_ref.at[0, 0, pl.ds(idx0, n_rows)]

    def start(r, priority):
        t = ids[r]
        pltpu.make_async_copy(src_hbm.at[t], dst.at[row0 + r], sem).start(priority=priority)

    if unrolled:
        for r in range(n_rows):
            start(r, r % 2)
    else:
        def body(g, carry):
            for j in range(GATHER_UNROLL):
                start(g * GATHER_UNROLL + j, j % 2)
            return carry
        lax.fori_loop(0, n_rows // GATHER_UNROLL, body, 0)


def _wait_row_gather(n_rows, src_hbm, dst, row0, sem):
    pltpu.make_async_copy(src_hbm.at[pl.ds(0, n_rows)], dst.at[pl.ds(row0, n_rows)], sem).wait()


GATHER_SLOTS = 3
BLOCK_GROUP = 8


def _expert_kernel(blk_ref, e_ref, lo_ref, hi_ref, tok_ref, gate_ref, x_hbm, wg_ref, wu_ref, wd_ref,
                   y_ref, buf, sem, wgb, wub, wdb):
    i = pl.program_id(0)
    n = pl.num_programs(0)
    rows = MOE_BLOCK

    def in_group(d):
        return (blk_ref[jnp.minimum(i + d, n - 1)] % BLOCK_GROUP) * rows

    @pl.when(i == 0)
    def _():
        _start_row_gather(tok_ref, rows, x_hbm, buf, 0, sem.at[0], idx0=in_group(0))
        _start_row_gather(tok_ref, rows, x_hbm, buf, rows, sem.at[1], idx0=in_group(1))

    @pl.when(jnp.logical_or(i == 0, blk_ref[i] != blk_ref[jnp.maximum(i - 1, 0)]))
    def _():
        y_ref[...] = jnp.zeros_like(y_ref)

    @pl.when(jnp.logical_or(i == 0, e_ref[i] != e_ref[jnp.maximum(i - 1, 0)]))
    def _():
        wgb[...] = wg_ref[...].astype(BF16)
        wub[...] = wu_ref[...].astype(BF16)
        wdb[...] = wd_ref[...].astype(BF16)

    slot = i % GATHER_SLOTS
    row0 = pl.multiple_of(slot * rows, rows)
    _wait_row_gather(rows, x_hbm, buf, row0, sem.at[slot])
    xg = _from_token_tiles(buf[pl.ds(row0, rows)]).astype(BF16)
    nslot = (i + 2) % GATHER_SLOTS
    _start_row_gather(tok_ref, rows, x_hbm, buf, nslot * rows, sem.at[nslot], unrolled=True, idx0=in_group(2))
    a = jnp.dot(xg, wgb[...], preferred_element_type=F32)
    u = jnp.dot(xg, wub[...], preferred_element_type=F32)
    h = (a / (1.0 + jnp.exp(-a)) * u).astype(BF16)
    y = jnp.dot(h, wdb[...], preferred_element_type=F32)
    r = lax.broadcasted_iota(jnp.int32, (rows, 1), 0)
    mine = jnp.logical_and(r >= lo_ref[i], r < hi_ref[i])
    grow = gate_ref[0, pl.ds(blk_ref[i] % BLOCK_GROUP, 1), :]
    eye = lax.broadcasted_iota(jnp.int32, (rows, rows), 0) == lax.broadcasted_iota(jnp.int32, (rows, rows), 1)
    gate = jnp.sum(jnp.where(eye, grow, 0.0), axis=1, keepdims=True)
    y_ref[...] += _to_token_tiles(jnp.where(mine, y * gate, 0.0))

    @pl.when(i == n - 1)
    def _():
        for d in (1, 2):
            s = (i + d) % GATHER_SLOTS
            _wait_row_gather(rows, x_hbm, buf, s * rows, sem.at[s])


def _experts(items, tok_blocks, gate_blocks, x1, w_gate, w_up, w_down):
    it_blk, it_e, it_lo, it_hi = items
    NI = it_blk.shape[0]
    NBLK = tok_blocks.shape[0] * BLOCK_GROUP
    ahead = lambda d: (lambda i, blk, e, lo, hi: (blk[jnp.minimum(i + d, NI - 1)] // BLOCK_GROUP, 0, 0))
    wsel = lambda i, blk, e, lo, hi: (e[i], 0, 0)
    group_rows = BLOCK_GROUP * MOE_BLOCK
    grid_spec = pltpu.PrefetchScalarGridSpec(
        num_scalar_prefetch=4,
        grid=(NI,),
        in_specs=[
            pl.BlockSpec((1, 1, group_rows), ahead(2), memory_space=pltpu.SMEM),
            pl.BlockSpec((1, BLOCK_GROUP, MOE_BLOCK), ahead(0)),
            pl.BlockSpec(memory_space=pl.ANY),
            pl.BlockSpec((None, D_MODEL, EXPERT_FF), wsel),
            pl.BlockSpec((None, D_MODEL, EXPERT_FF), wsel),
            pl.BlockSpec((None, EXPERT_FF, D_MODEL), wsel),
        ],
        out_specs=pl.BlockSpec((MOE_BLOCK, SUBLANES, LANES), lambda i, blk, e, lo, hi: (blk[i], 0, 0)),
        scratch_shapes=[pltpu.VMEM((GATHER_SLOTS * MOE_BLOCK, SUBLANES, LANES), F32),
                        pltpu.SemaphoreType.DMA((GATHER_SLOTS,)),
                        pltpu.VMEM((D_MODEL, EXPERT_FF), BF16), pltpu.VMEM((D_MODEL, EXPERT_FF), BF16),
                        pltpu.VMEM((EXPERT_FF, D_MODEL), BF16)],
    )
    return pl.pallas_call(
        _expert_kernel,
        grid_spec=grid_spec,
        out_shape=jax.ShapeDtypeStruct((NBLK * MOE_BLOCK, SUBLANES, LANES), F32),
        compiler_params=_cparams(48),
        name="moe_experts",
    )(it_blk, it_e, it_lo, it_hi, tok_blocks, gate_blocks, x1, w_gate, w_up, w_down)


def _combine_kernel(posc_ref, posn_ref, ys_hbm, x1_ref, wsg_ref, wsu_ref, wsd_ref, g_ref, b_ref, out_ref, buf, sem):
    i = pl.program_id(0)
    n = pl.num_programs(0)
    tt = x1_ref.shape[0]
    rows = TOP_K * tt
    slot = i % 2

    @pl.when(i == 0)
    def _():
        _start_row_gather(posc_ref, rows, ys_hbm, buf, 0, sem.at[0])

    @pl.when(i + 1 < n)
    def _():
        nslot = (i + 1) % 2
        _start_row_gather(posn_ref, rows, ys_hbm, buf, nslot * rows, sem.at[nslot])

    x1 = x1_ref[...]
    xb = x1.astype(BF16)
    a = jnp.dot(xb, wsg_ref[...], preferred_element_type=F32)
    u = jnp.dot(xb, wsu_ref[...], preferred_element_type=F32)
    h = (a / (1.0 + jnp.exp(-a)) * u).astype(BF16)
    ffn = jnp.dot(h, wsd_ref[...], preferred_element_type=F32)

    row0 = pl.multiple_of(slot * rows, rows)
    _wait_row_gather(rows, ys_hbm, buf, row0, sem.at[slot])
    routed = buf[pl.ds(row0, tt)]
    for k in range(1, TOP_K):
        routed = routed + buf[pl.ds(row0 + k * tt, tt)]
    ffn = ffn + _from_token_tiles(routed)
    out_ref[...] = _layer_norm(DEEPNORM_ALPHA * x1 + ffn, g_ref[...], b_ref[...])


def _combine(pos_blocks, ys, x1, ws_gate, ws_up, ws_down, g, b, tt=128):
    T = x1.shape[0]
    n = T // tt
    rows = TOP_K * tt
    row = lambda i: (i, 0)
    fix = lambda i: (0, 0)
    return pl.pallas_call(
        _combine_kernel,
        grid=(n,),
        in_specs=[
            pl.BlockSpec((1, 1, rows), lambda i: (i, 0, 0), memory_space=pltpu.SMEM),
            pl.BlockSpec((1, 1, rows), lambda i: (jnp.minimum(i + 1, n - 1), 0, 0), memory_space=pltpu.SMEM),
            pl.BlockSpec(memory_space=pl.ANY),
            pl.BlockSpec((tt, D_MODEL), row),
            pl.BlockSpec((D_MODEL, SHARED_FF), fix),
            pl.BlockSpec((D_MODEL, SHARED_FF), fix),
            pl.BlockSpec((SHARED_FF, D_MODEL), fix),
            pl.BlockSpec((1, D_MODEL), fix),
            pl.BlockSpec((1, D_MODEL), fix),
        ],
        out_specs=pl.BlockSpec((tt, D_MODEL), row),
        out_shape=jax.ShapeDtypeStruct((T, D_MODEL), F32),
        scratch_shapes=[pltpu.VMEM((2 * rows, SUBLANES, LANES), F32), pltpu.SemaphoreType.DMA((2,))],
        compiler_params=_cparams(56),
        name="moe_combine_ln2",
    )(pos_blocks, pos_blocks, ys, x1, ws_gate, ws_up, ws_down, g, b)


def _rope_tables(T):
    inv = 1.0 / (ROPE_THETA ** (jnp.arange(0, HEAD_DIM, 2, dtype=F32) / HEAD_DIM))
    ang = jnp.arange(T, dtype=F32)[:, None] * inv[None, :]
    cos, sin = jnp.cos(ang), jnp.sin(ang)
    cos128 = jnp.tile(cos, (1, 4))
    sin128 = jnp.tile(jnp.concatenate([-sin, sin], axis=1), (1, 2))
    return cos128, sin128


def _swa_head_order(a, axis):
    half = SWA_Q_HEADS // 2
    shp = a.shape
    a = a.reshape(shp[:axis] + (2, half, HEAD_DIM) + shp[axis + 1:])
    return jnp.swapaxes(a, axis, axis + 1).reshape(shp)


def _item_tables(top_e, gate, counts, T):
    A = T * TOP_K
    NBLK = A // MOE_BLOCK
    NI = NBLK + N_EXPERTS - 1
    starts = jnp.cumsum(counts) - counts
    ends = starts + counts
    first_blk = starts // MOE_BLOCK
    nblk = jnp.where(counts > 0, (ends - 1) // MOE_BLOCK - first_blk + 1, 0)
    item_end = jnp.cumsum(nblk)
    item_off = item_end - nblk
    it = jnp.arange(NI, dtype=jnp.int32)
    live = it < item_end[-1]
    it_e = jnp.minimum(jnp.sum((item_end[None, :] <= it[:, None]).astype(jnp.int32), axis=1), N_EXPERTS - 1)
    onehot = (it_e[:, None] == jnp.arange(N_EXPERTS, dtype=jnp.int32)[None, :]).astype(jnp.int32)
    pick = lambda v: jnp.sum(onehot * v[None, :].astype(jnp.int32), axis=1)
    it_blk = pick(first_blk) + it - pick(item_off)
    it_lo = jnp.clip(pick(starts) - it_blk * MOE_BLOCK, 0, MOE_BLOCK)
    it_hi = jnp.clip(pick(ends) - it_blk * MOE_BLOCK, 0, MOE_BLOCK)
    it_blk = jnp.where(live, it_blk, NBLK - 1)
    it_lo = jnp.where(live, it_lo, 0)
    it_hi = jnp.where(live, it_hi, 0)
    tok = jnp.arange(T, dtype=jnp.int32)[:, None]
    key_s, gate_s = lax.sort(((top_e * T + tok).reshape(A), gate.reshape(A)), num_keys=1)
    items = tuple(v.astype(jnp.int32) for v in (it_blk, it_e, it_lo, it_hi))
    group_rows = BLOCK_GROUP * MOE_BLOCK
    return (items, (key_s % T).reshape(A // group_rows, 1, group_rows),
            gate_s.reshape(A // group_rows, BLOCK_GROUP, MOE_BLOCK), starts)


def kernel(x, w_in, b_in, sinks, lam_q1, lam_k1, lam_q2, lam_k2, diff_norm_g, w_out, ln1_g, ln1_b,
           w_router, router_bias, w_gate, w_up, w_down, ws_gate, ws_up, ws_down, ln2_g, ln2_b):
    B, S, D = x.shape
    assert D == D_MODEL and w_in.shape[0] == DEPTH
    T = B * S
    assert B == 1 and T % 512 == 0
    cos128, sin128 = _rope_tables(S)
    x2 = x.reshape(T, D)
    for l in range(DEPTH):
        lam_init = 0.8 - 0.6 * math.exp(-0.3 * l)
        wi = w_in[l]
        wi = jnp.concatenate([_swa_head_order(wi[:, :_AK], 1), wi[:, _AK:]], axis=1).astype(BF16)
        bi = jnp.concatenate([_swa_head_order(b_in[l][:_AK], 0), b_in[l][_AK:]])[None, :]
        wo = jnp.concatenate([_swa_head_order(w_out[l][:_AK], 0), w_out[l][_AK:]], axis=0).astype(BF16)
        sink_p = sinks[l].astype(F32)
        lam = (jnp.exp(jnp.sum(lam_q1[l].astype(F32) * lam_k1[l].astype(F32)))
               - jnp.exp(jnp.sum(lam_q2[l].astype(F32) * lam_k2[l].astype(F32))) + lam_init).reshape(1)
        wr = w_router[l]
        wr_hi = wr.astype(BF16)
        wr_lo = (wr - wr_hi.astype(F32)).astype(BF16)

        qa, ka, va, qd, kd, vd = _in_proj(x2, wi, bi, cos128, sin128)
        oa = _swa(sink_p, qa, ka, va)
        ob = _diff_attn(lam, qd, kd, vd, diff_norm_g[l][None, :], lam_init)
        x1, x1t, logits = _out_proj(oa, ob, x2, wo, ln1_g[l][None, :], ln1_b[l][None, :], wr_hi, wr_lo)
        idx, gates, rank, counts = _route(logits, router_bias[l][None, :])
        items, tok_blocks, gate_blocks, starts = _item_tables(idx[:, :TOP_K], gates[:, :TOP_K], counts[0], T)
        pos = _row_pos(idx, rank, starts.astype(F32)[None, :])[:, :TOP_K]
        ys = _experts(items, tok_blocks, gate_blocks, x1t, w_gate[l], w_up[l], w_down[l])
        tt = 512
        pos_blocks = pos.reshape(T // tt, tt, TOP_K).transpose(0, 2, 1).reshape(T // tt, 1, TOP_K * tt)
        x2 = _combine(pos_blocks, ys, x1, ws_gate[l].astype(BF16), ws_up[l].astype(BF16),
                      ws_down[l].astype(BF16), ln2_g[l][None, :], ln2_b[l][None, :], tt=tt)
    return x2.reshape(B, S, D)
```
